```python
import math, functools
import jax
import jax.numpy as jnp
from jax import lax
import numpy as np

D_MODEL = 1024
BATCH = 16
SEQ = 2048
DEPTH = 4

GRID_W = 64
CTX_LEN = 256
N_MIXERS = 4
F32 = jnp.float32
RMS_EPS = 1e-6
L2_EPS = 1e-6
ROPE_BASE = 10000.0

NA_HEADS = 16
NA_HEAD_DIM = D_MODEL // NA_HEADS
NA_WIN_ROWS = 8
NA_WIN_COLS = 16
GDN_HEADS = 8
GDN_HEAD_DIM = D_MODEL // GDN_HEADS
GDN_WIDTH = GDN_HEADS * GDN_HEAD_DIM
GDN_CONV = 3
GDN_CHUNK = 64
DIFF_HEADS = 8
DIFF_HEAD_DIM = D_MODEL // (2 * DIFF_HEADS)
DIFF_Q_BLOCK = 128
HY_ORDER = 2
HY_CONV = 3
HY_EMB_DIM = 33
HY_FILTER_WIDTH = 64
HY_MAX_DECAY = math.log(1e-2) / 0.3
HY_MIN_DECAY = math.log(1e-2) / 1.5
FFN_DIM = 2816
MOE_EXPERTS = 8
MOE_TOP_K = 2
MOE_FFN_DIM = 1408

kernel_name = 'hybrid_latent_diffusion_block'


def _rmsnorm(x, g):
    x32 = x.astype(F32)
    y = x32 * lax.rsqrt(jnp.mean(x32 * x32, axis=-1, keepdims=True) + RMS_EPS)
    return y.astype(x.dtype) * g


def _l2norm(x):
    x32 = x.astype(F32)
    return x32 * lax.rsqrt(jnp.sum(x32 * x32, axis=-1, keepdims=True) + L2_EPS)


def _modulate(h, shift, scale):
    return h * (1.0 + scale) + shift


def _dwconv(x, w):
    k, ch = w.shape
    return lax.conv_general_dilated(x, w.astype(x.dtype)[:, None, :], window_strides=(1,),
                                    padding=[(k // 2, k // 2)],
                                    dimension_numbers=('NWC', 'WIO', 'NWC'),
                                    feature_group_count=ch)


def _axial_rope(x, rows, cols):
    half = x.shape[-1] // 2
    inv = ROPE_BASE ** (-jnp.arange(0, half, 2, dtype=F32) / half)
    bshape = (x.shape[1],) + (1,) * (x.ndim - 3) + (half,)

    def rot(xa, p):
        ang = p.astype(F32)[:, None] * inv[None, :]
        ang = jnp.concatenate([ang, ang], axis=-1).reshape(bshape)
        xa = xa.astype(F32)
        x1, x2 = jnp.split(xa, 2, axis=-1)
        return xa * jnp.cos(ang) + jnp.concatenate([-x2, x1], axis=-1) * jnp.sin(ang)

    return jnp.concatenate([rot(x[..., :half], rows), rot(x[..., half:], cols)], axis=-1).astype(x.dtype)


def _neighbourhood_attention(hl, hc, w_qkv, q_norm, k_norm, rpb, w_o):
    B, S, D = hl.shape
    R = S // GRID_W
    wr = min(NA_WIN_ROWS, R)
    scale = NA_HEAD_DIM ** -0.5

    def heads(h):
        q, k, v = jnp.split(h @ w_qkv, 3, axis=-1)
        shp = h.shape[:2] + (NA_HEADS, NA_HEAD_DIM)
        return _rmsnorm(q.reshape(shp), q_norm) * scale, _rmsnorm(k.reshape(shp), k_norm), v.reshape(shp)

    q, k, v = heads(hl)
    qc, kc, vc = heads(hc)
    pc = jax.nn.softmax(jnp.einsum('bqhd,bkhd->bhqk', qc, kc).astype(F32), axis=-1).astype(vc.dtype)
    yc = jnp.einsum('bhqk,bkhd->bqhd', pc, vc).reshape(hc.shape) @ w_o

    grid = lambda t: t.reshape(B, R, GRID_W, NA_HEADS, NA_HEAD_DIM)
    qg, kg, vg = grid(q), grid(k), grid(v)
    cols = jnp.arange(GRID_W)
    c0 = jnp.clip(cols - NA_WIN_COLS // 2, 0, GRID_W - NA_WIN_COLS)
    col_valid = (cols[None, :] >= c0[:, None]) & (cols[None, :] < c0[:, None] + NA_WIN_COLS)
    col_idx = jnp.clip(cols[None, :] - cols[:, None], 1 - NA_WIN_COLS, NA_WIN_COLS - 1) + NA_WIN_COLS - 1
    n_loc = wr * GRID_W

    def row_block(r):
        r0 = jnp.clip(r - wr // 2, 0, R - wr)
        q_r = lax.dynamic_index_in_dim(qg, r, axis=1, keepdims=False)
        k_r = lax.dynamic_slice_in_dim(kg, r0, wr, axis=1)
        v_r = lax.dynamic_slice_in_dim(vg, r0, wr, axis=1)
        row_idx = r0 + jnp.arange(wr) - r + NA_WIN_ROWS - 1
        bias = jnp.transpose(rpb[:, row_idx][:, :, col_idx], (0, 2, 1, 3))
        s_loc = jnp.einsum('bqhd,bikhd->bhqik', q_r, k_r).astype(F32) + bias
        s_loc = jnp.where(col_valid[:, None, :], s_loc, -jnp.inf).reshape(B, NA_HEADS, GRID_W, n_loc)
        s_ctx = jnp.einsum('bqhd,bchd->bhqc', q_r, kc).astype(F32)
        p = jax.nn.softmax(jnp.concatenate([s_loc, s_ctx], axis=-1), axis=-1).astype(v.dtype)
        p_loc = p[..., :n_loc].reshape(B, NA_HEADS, GRID_W, wr, GRID_W)
        return (jnp.einsum('bhqik,bikhd->bqhd', p_loc, v_r)
                + jnp.einsum('bhqc,bchd->bqhd', p[..., n_loc:], vc))

    o = lax.map(row_block, jnp.arange(R))
    yl = jnp.moveaxis(o, 0, 1).reshape(B, S, D) @ w_o
    return yl, yc


def _chunk_gated_delta(q, k, v, g, beta, state0):
    B, H, L, dk = q.shape
    dv = v.shape[-1]
    n = L // GDN_CHUNK
    C = GDN_CHUNK
    q = (q * dk ** -0.5).reshape(B, H, n, C, dk)
    k = k.reshape(B, H, n, C, dk)
    v = v.reshape(B, H, n, C, dv)
    beta = beta.reshape(B, H, n, C)
    gc = jnp.cumsum(g.reshape(B, H, n, C), axis=-1)
    idx = jnp.arange(C)
    incl = idx[:, None] >= idx[None, :]
    decay = jnp.exp(jnp.where(incl, gc[..., :, None] - gc[..., None, :], -jnp.inf))
    kb = k * beta[..., None]
    a_strict = jnp.einsum('bhnid,bhnjd->bhnij', kb, k) * decay * (idx[:, None] > idx[None, :])
    rhs = jnp.concatenate([v * beta[..., None], kb * jnp.exp(gc)[..., None]], axis=-1)
    sol = lax.linalg.triangular_solve(a_strict + jnp.eye(C, dtype=F32), rhs, left_side=True,
                                      lower=True, unit_diagonal=True)
    u, w = sol[..., :dv], sol[..., dv:]
    attn = jnp.einsum('bhnid,bhnjd->bhnij', q, k) * decay

    def step(S, inp):
        q_i, k_i, u_i, w_i, g_i, a_i = inp
        v_new = u_i - jnp.einsum('bhcd,bhde->bhce', w_i, S)
        o = (jnp.einsum('bhcd,bhde->bhce', q_i * jnp.exp(g_i)[..., None], S)
             + jnp.einsum('bhij,bhje->bhie', a_i, v_new))
        g_last = g_i[..., -1]
        S = (S * jnp.exp(g_last)[..., None, None]
             + jnp.einsum('bhcd,bhce->bhde', k_i * jnp.exp(g_last[..., None] - g_i)[..., None], v_new))
        return S, o

    xs = tuple(jnp.moveaxis(t, 2, 0) for t in (q, k, u, w, gc, attn))
    S, o = lax.scan(step, state0, xs)
    return jnp.moveaxis(o, 0, 2).reshape(B, H, L, dv), S


def _gated_deltanet(hl, hc, w_qkv, conv_w, w_gate, w_beta, w_decay, a_log, dt_bias, out_norm, w_o):
    def bidir(h, s0_f, s0_b):
        B, L, _ = h.shape
        u = jax.nn.silu(_dwconv(h @ w_qkv, conv_w))
        heads = lambda t: jnp.swapaxes(t.reshape(B, L, GDN_HEADS, GDN_HEAD_DIM), 1, 2)
        q, k, v = jnp.split(u, 3, axis=-1)
        q, k, v = _l2norm(heads(q)), _l2norm(heads(k)), heads(v).astype(F32)

        def gates(d):
            beta = jax.nn.sigmoid((h @ w_beta[d]).astype(F32))
            g = -jnp.exp(a_log[d].astype(F32)) * jax.nn.softplus((h @ w_decay[d]).astype(F32) + dt_bias[d])
            return jnp.swapaxes(g, 1, 2), jnp.swapaxes(beta, 1, 2)

        flip = lambda t: jnp.flip(t, axis=2)
        g_f, b_f = gates(0)
        g_b, b_b = gates(1)
        o_f, s_f = _chunk_gated_delta(q, k, v, g_f, b_f, s0_f)
        o_b, s_b = _chunk_gated_delta(flip(q), flip(k), flip(v), flip(g_b), flip(b_b), s0_b)
        o = jnp.swapaxes(o_f + flip(o_b), 1, 2)
        gate = jax.nn.silu(h @ w_gate).reshape(B, L, GDN_HEADS, GDN_HEAD_DIM)
        y = (_rmsnorm(o, out_norm) * gate).reshape(B, L, GDN_WIDTH) @ w_o
        return y, s_f, s_b

    zeros = jnp.zeros((hc.shape[0], GDN_HEADS, GDN_HEAD_DIM, GDN_HEAD_DIM), F32)
    yc, sc_f, sc_b = bidir(hc, zeros, zeros)
    yl, _, _ = bidir(hl, sc_f, sc_b)
    return yl, yc


def _diff_weights(s, lam):
    p = jax.nn.softmax(s, axis=-1)
    return p[:, 0] - lam * p[:, 1]


def _diff_attention(hl, hc, lambda_init, w_qkv, q_norm, k_norm, lam_vecs, out_norm, w_o):
    B, S, D = hl.shape
    hd = DIFF_HEAD_DIM
    lv = lam_vecs.astype(F32)
    lam = jnp.exp(jnp.sum(lv[0] * lv[1])) - jnp.exp(jnp.sum(lv[2] * lv[3])) + lambda_init

    def heads(h):
        q, k, v = jnp.split(h @ w_qkv, 3, axis=-1)
        shp = h.shape[:2] + (DIFF_HEADS, 2, hd)
        return (_rmsnorm(q.reshape(shp), q_norm), _rmsnorm(k.reshape(shp), k_norm),
                v.reshape(h.shape[:2] + (DIFF_HEADS, 2 * hd)))

    pos = jnp.arange(S)
    rows, cols = pos // GRID_W, pos % GRID_W
    q, k, v = heads(hl)
    q = _axial_rope(q, rows, cols) * hd ** -0.5
    k = _axial_rope(k, rows, cols)
    qc, kc, vc = heads(hc)
    qc = qc * hd ** -0.5
    ac = _diff_weights(jnp.einsum('bqhmd,bkhmd->bmhqk', qc, kc).astype(F32), lam).astype(vc.dtype)
    oc = jnp.einsum('bhqk,bkhe->bqhe', ac, vc)

    nb = S // DIFF_Q_BLOCK
    qb = jnp.moveaxis(q.reshape(B, nb, DIFF_Q_BLOCK, DIFF_HEADS, 2, hd), 1, 0)

    def block(q_blk):
        s = jnp.concatenate([jnp.einsum('bqhmd,bkhmd->bmhqk', q_blk, k),
                             jnp.einsum('bqhmd,bkhmd->bmhqk', q_blk, kc)], axis=-1).astype(F32)
        a = _diff_weights(s, lam).astype(v.dtype)
        return (jnp.einsum('bhqk,bkhe->bqhe', a[..., :S], v)
                + jnp.einsum('bhqc,bche->bqhe', a[..., S:], vc))

    ol = jnp.moveaxis(lax.map(block, qb), 0, 1).reshape(B, S, DIFF_HEADS, 2 * hd)
    out = lambda o: (_rmsnorm(o, out_norm) * (1.0 - lambda_init)).reshape(o.shape[:2] + (D,)) @ w_o
    return out(ol), out(oc)


def _hyena_filters(L, w1, b1, freq, w2, b2, w3, b3):
    D = w3.shape[1] // (2 * HY_ORDER)
    t = jnp.linspace(0.0, 1.0, L, dtype=F32)[:, None]
    bands = (HY_EMB_DIM - 1) // 2
    w = 2.0 * math.pi * jnp.arange(L, dtype=F32) / L
    f = jnp.linspace(1e-4, bands - 1, bands, dtype=F32)
    ang = w[:, None] * f[None, :]
    z = jnp.concatenate([t, jnp.cos(ang), -jnp.sin(ang)], axis=-1)
    hdn = jnp.sin(freq[0] * (z @ w1 + b1))
    hdn = jnp.sin(freq[1] * (hdn @ w2 + b2))
    h = (hdn @ w3 + b3).astype(F32).reshape(L, HY_ORDER, 2, D)
    deltas = jnp.abs(jnp.linspace(HY_MIN_DECAY, HY_MAX_DECAY, D, dtype=F32))
    h = h * jnp.exp(-t * deltas)[:, None, None, :]
    return h / jnp.sum(jnp.abs(h), axis=(0, 2), keepdims=True)


def _bidir_fftconv(u, h_fwd, h_bwd, skip):
    L = u.shape[1]
    u32 = u.astype(F32)
    kern = jnp.concatenate([h_fwd, jnp.zeros_like(h_fwd[:1]), h_bwd[:0:-1]], axis=0)
    y = jnp.fft.irfft(jnp.fft.rfft(u32, n=2 * L, axis=1) * jnp.fft.rfft(kern, n=2 * L, axis=0)[None],
                      n=2 * L, axis=1)[:, :L]
    return (y + u32 * skip).astype(u.dtype)


def _hyena(h, w_in, conv_w, f_w1, f_b1, f_freq, f_w2, f_b2, f_w3, f_b3, skip, w_o):
    L = h.shape[1]
    x1, x2, v = jnp.split(_dwconv(h @ w_in, conv_w), HY_ORDER + 1, axis=-1)
    filt = _hyena_filters(L, f_w1, f_b1, f_freq, f_w2, f_b2, f_w3, f_b3)
    z = v
    for n, gate in enumerate((x1, x2)):
        z = gate * _bidir_fftconv(z, filt[:, n, 0], filt[:, n, 1], skip[n])
    return z @ w_o


def _swiglu(h, w_in, w_out):
    a, b = jnp.split(h @ w_in, 2, axis=-1)
    return (jax.nn.silu(a) * b) @ w_out


def _moe(h, router, w_in, w_out):
    logits = (h @ router).astype(F32)
    top_v, top_i = lax.top_k(logits, MOE_TOP_K)
    top_w = jax.nn.softmax(top_v, axis=-1)
    gates = jnp.sum(jax.nn.one_hot(top_i, MOE_EXPERTS, dtype=F32) * top_w[..., None], axis=-2).astype(h.dtype)
    return sum(gates[..., e, None] * _swiglu(h, w_in[e], w_out[e]) for e in range(MOE_EXPERTS))


def setup_inputs(seed: int = 0) -> dict:
    key = jax.random.key(seed)
    keys = iter(jax.random.split(key, 64))
    D = D_MODEL
    n_dense, n_moe = (DEPTH + 1) // 2, DEPTH // 2

    def nrm(shape, scale):
        return jax.random.normal(next(keys), shape, F32) * scale

    def gain(shape):
        return 1.0 + nrm(shape, 0.05)

    dt = jnp.exp(jax.random.uniform(next(keys), (2, GDN_HEADS), F32, math.log(1e-3), math.log(1e-1)))
    a_log = jnp.log(jax.random.uniform(next(keys), (2, GDN_HEADS), F32, 1.0, 16.0))
    return {
        'x': nrm((BATCH, SEQ, D), 1.0),
        'c': nrm((BATCH, D), 1.0),
        'ctx': nrm((BATCH, CTX_LEN, D), 1.0),
        'c_ctx': nrm((D,), 1.0),
        'ada_w': nrm((DEPTH, D, 6 * D), 0.5 * D ** -0.5),
        'ada_b': nrm((DEPTH, 6 * D), 0.02),
        'norm_g': gain((DEPTH, 2, D)),
        'na_w_qkv': nrm((D, 3 * D), D ** -0.5),
        'na_q_norm': gain((NA_HEAD_DIM,)),
        'na_k_norm': gain((NA_HEAD_DIM,)),
        'na_rpb': nrm((NA_HEADS, 2 * NA_WIN_ROWS - 1, 2 * NA_WIN_COLS - 1), 0.1),
        'na_w_o': nrm((D, D), D ** -0.5),
        'gdn_w_qkv': nrm((D, 3 * GDN_WIDTH), D ** -0.5),
        'gdn_conv': nrm((GDN_CONV, 3 * GDN_WIDTH), GDN_CONV ** -0.5),
        'gdn_w_gate': nrm((D, GDN_WIDTH), D ** -0.5),
        'gdn_w_beta': nrm((2, D, GDN_HEADS), D ** -0.5),
        'gdn_w_decay': nrm((2, D, GDN_HEADS), 0.1 * D ** -0.5),
        'gdn_a_log': a_log,
        'gdn_dt_bias': dt + jnp.log(-jnp.expm1(-dt)),
        'gdn_out_norm': gain((GDN_HEAD_DIM,)),
        'gdn_w_o': nrm((GDN_WIDTH, D), GDN_WIDTH ** -0.5),
        'diff_w_qkv': nrm((D, 3 * D), D ** -0.5),
        'diff_q_norm': gain((DIFF_HEAD_DIM,)),
        'diff_k_norm': gain((DIFF_HEAD_DIM,)),
        'diff_lambda': nrm((4, DIFF_HEAD_DIM), 0.1),
        'diff_out_norm': gain((2 * DIFF_HEAD_DIM,)),
        'diff_w_o': nrm((D, D), D ** -0.5),
        'hy_w_in': nrm((D, (HY_ORDER + 1) * D), D ** -0.5),
        'hy_conv': nrm((HY_CONV, (HY_ORDER + 1) * D), HY_CONV ** -0.5),
        'hy_filt_w1': nrm((HY_EMB_DIM, HY_FILTER_WIDTH), HY_EMB_DIM ** -0.5),
        'hy_filt_b1': nrm((HY_FILTER_WIDTH,), 0.1),
        'hy_filt_freq': gain((2, HY_FILTER_WIDTH)),
        'hy_filt_w2': nrm((HY_FILTER_WIDTH, HY_FILTER_WIDTH), HY_FILTER_WIDTH ** -0.5),
        'hy_filt_b2': nrm((HY_FILTER_WIDTH,), 0.1),
        'hy_filt_w3': nrm((HY_FILTER_WIDTH, 2 * HY_ORDER * D), HY_FILTER_WIDTH ** -0.5),
        'hy_filt_b3': nrm((2 * HY_ORDER * D,), 0.02),
        'hy_skip': nrm((HY_ORDER, D), 1.0),
        'hy_w_o': nrm((D, D), D ** -0.5),
        'ffn_w_in': nrm((n_dense, D, 2 * FFN_DIM), D ** -0.5),
        'ffn_w_out': nrm((n_dense, FFN_DIM, D), FFN_DIM ** -0.5),
        'moe_router': nrm((n_moe, D, MOE_EXPERTS), D ** -0.5),
        'moe_w_in': nrm((n_moe, MOE_EXPERTS, D, 2 * MOE_FFN_DIM), D ** -0.5),
        'moe_w_out': nrm((n_moe, MOE_EXPERTS, MOE_FFN_DIM, D), MOE_FFN_DIM ** -0.5),
    }


def reference(x, c, ctx, c_ctx, ada_w, ada_b, norm_g,
              na_w_qkv, na_q_norm, na_k_norm, na_rpb, na_w_o,
              gdn_w_qkv, gdn_conv, gdn_w_gate, gdn_w_beta, gdn_w_decay, gdn_a_log, gdn_dt_bias,
              gdn_out_norm, gdn_w_o,
              diff_w_qkv, diff_q_norm, diff_k_norm, diff_lambda, diff_out_norm, diff_w_o,
              hy_w_in, hy_conv, hy_filt_w1, hy_filt_b1, hy_filt_freq, hy_filt_w2, hy_filt_b2,
              hy_filt_w3, hy_filt_b3, hy_skip, hy_w_o,
              ffn_w_in, ffn_w_out, moe_router, moe_w_in, moe_w_out):
    n_ctx = ctx.shape[1]
    silu_c = jax.nn.silu(c)
    silu_cc = jax.nn.silu(c_ctx)
    hyena = functools.partial(_hyena, w_in=hy_w_in, conv_w=hy_conv, f_w1=hy_filt_w1, f_b1=hy_filt_b1,
                              f_freq=hy_filt_freq, f_w2=hy_filt_w2, f_b2=hy_filt_b2, f_w3=hy_filt_w3,
                              f_b3=hy_filt_b3, skip=hy_skip, w_o=hy_w_o)
    xl, xc = x, ctx
    for i in range(DEPTH):
        last = i == DEPTH - 1
        kind = i % N_MIXERS
        ml = [m[:, None, :] for m in jnp.split(silu_c @ ada_w[i] + ada_b[i], 6, axis=-1)]
        mc = jnp.split(silu_cc @ ada_w[i] + ada_b[i], 6, axis=-1)
        hl = _modulate(_rmsnorm(xl, norm_g[i, 0]), ml[0], ml[1])
        ctx_needed = (not last) or kind != 3
        hc = _modulate(_rmsnorm(xc, norm_g[i, 0]), mc[0], mc[1]) if ctx_needed else None
        if kind == 0:
            yl, yc = _neighbourhood_attention(hl, hc, na_w_qkv, na_q_norm, na_k_norm, na_rpb, na_w_o)
        elif kind == 1:
            yl, yc = _gated_deltanet(hl, hc, gdn_w_qkv, gdn_conv, gdn_w_gate, gdn_w_beta, gdn_w_decay,
                                     gdn_a_log, gdn_dt_bias, gdn_out_norm, gdn_w_o)
        elif kind == 2:
            yl, yc = _diff_attention(hl, hc, 0.8 - 0.6 * math.exp(-0.3 * i), diff_w_qkv, diff_q_norm,
                                     diff_k_norm, diff_lambda, diff_out_norm, diff_w_o)
        else:
            yl = hyena(hl)
            yc = hyena(hc) if ctx_needed else None
        xl = xl + ml[2] * yl
        if not last:
            xc = xc + mc[2] * yc
        j = i // 2
        if i % 2 == 0:
            ffn = functools.partial(_swiglu, w_in=ffn_w_in[j], w_out=ffn_w_out[j])
        else:
            ffn = functools.partial(_moe, router=moe_router[j], w_in=moe_w_in[j], w_out=moe_w_out[j])
        hl = _modulate(_rmsnorm(xl, norm_g[i, 1]), ml[3], ml[4])
        if last:
            xl = xl + ml[5] * ffn(hl)
        else:
            hc = _modulate(_rmsnorm(xc, norm_g[i, 1]), mc[3], mc[4])
            y = ffn(jnp.concatenate([hc, hl], axis=1))
            xc = xc + mc[5] * y[:, :n_ctx]
            xl = xl + ml[5] * y[:, n_ctx:]
    return xl
```

```python
import functools
import math

import numpy as np
import jax
import jax.numpy as jnp
from jax import lax
from jax.experimental import pallas as pl
from jax.experimental.pallas import tpu as pltpu

F32 = jnp.float32
BF16 = jnp.bfloat16
HIGHEST = lax.Precision.HIGHEST

VMEM_LIMIT_BYTES = 56 * 1024 * 1024
LANES = 128

D_MODEL = 1024
GRID_W = 64
RMS_EPS = 1e-6
L2_EPS = 1e-6
ROPE_BASE = 10000.0
NA_HEADS = 16
NA_HEAD_DIM = 64
NA_WIN_ROWS = 8
NA_WIN_COLS = 16
GDN_HEADS = 8
GDN_HEAD_DIM = 128
GDN_CHUNK = 64
DIFF_HEADS = 8
DIFF_HEAD_DIM = 64
HY_ORDER = 2
HY_EMB_DIM = 33
HY_MAX_DECAY = math.log(1e-2) / 0.3
HY_MIN_DECAY = math.log(1e-2) / 1.5
MOE_EXPERTS = 8
NEG_BIG = -1e30


def _params(*sem):
    return pltpu.CompilerParams(dimension_semantics=sem, vmem_limit_bytes=VMEM_LIMIT_BYTES)


def _dot(a, b):
    return jnp.dot(a, b, preferred_element_type=F32)


def _dot_t(a, b):
    return lax.dot_general(a, b, (((1,), (1,)), ((), ())), preferred_element_type=F32)


def _silu(x):
    return x * (1.0 / (1.0 + jnp.exp(-x)))


def _ada_kernel(c_ref, w_ref, b_ref, o_ref):
    s = _silu(c_ref[...])
    o_ref[0] = jnp.dot(s, w_ref[0], preferred_element_type=F32, precision=HIGHEST) + b_ref[0]


def _ada_mods(cc, ada_w, ada_b):
    depth, d, n = ada_w.shape
    r = cc.shape[0]
    tn = 1024
    return pl.pallas_call(
        _ada_kernel,
        out_shape=jax.ShapeDtypeStruct((depth, r, n), F32),
        grid=(depth, n // tn),
        in_specs=[pl.BlockSpec((r, d), lambda i, j: (0, 0)),
                  pl.BlockSpec((1, d, tn), lambda i, j: (i, 0, j)),
                  pl.BlockSpec((1, 1, tn), lambda i, j: (i, 0, j))],
        out_specs=pl.BlockSpec((1, r, tn), lambda i, j: (i, 0, j)),
        compiler_params=_params("arbitrary", "arbitrary"),
        name="ada_mods",
    )(cc, ada_w, ada_b.reshape(depth, 1, n))


def _norm_mod(x_ref, g_ref, sh_ref, sc_ref):
    x = x_ref[...]
    y = x * lax.rsqrt(jnp.mean(x * x, axis=-1, keepdims=True) + RMS_EPS) * g_ref[...]
    return y * (1.0 + sc_ref[0]) + sh_ref[0]


def _norm_mod_kernel(x_ref, g_ref, sh_ref, sc_ref, o_ref):
    o_ref[...] = _norm_mod(x_ref, g_ref, sh_ref, sc_ref).astype(o_ref.dtype)


def _norm_mod_route_kernel(x_ref, g_ref, sh_ref, sc_ref, r_ref, o_ref, gate_ref):
    h = _norm_mod(x_ref, g_ref, sh_ref, sc_ref)
    o_ref[...] = h.astype(o_ref.dtype)
    logits = jnp.dot(h, r_ref[...], preferred_element_type=F32, precision=HIGHEST)
    lane = lax.broadcasted_iota(jnp.int32, logits.shape, 1).astype(F32)
    l1 = jnp.where(lane < MOE_EXPERTS, logits, NEG_BIG)
    m1 = jnp.max(l1, axis=-1, keepdims=True)
    i1 = jnp.min(jnp.where(l1 == m1, lane, float(LANES)), axis=-1, keepdims=True)
    l2 = jnp.where(lane == i1, NEG_BIG, l1)
    m2 = jnp.max(l2, axis=-1, keepdims=True)
    i2 = jnp.min(jnp.where(l2 == m2, lane, float(LANES)), axis=-1, keepdims=True)
    e2 = jnp.exp(m2 - m1)
    w1 = 1.0 / (1.0 + e2)
    gate_ref[...] = jnp.where(lane == i1, w1, 0.0) + jnp.where(lane == i2, e2 * w1, 0.0)


def _norm_modulate(x, g, shift, scale, router=None, tm=512):
    m, d = x.shape
    grp = shift.shape[0]
    rows = m // grp
    tm = min(tm, rows)
    assert rows % tm == 0
    in_specs = [pl.BlockSpec((tm, d), lambda i: (i, 0)),
                pl.BlockSpec((1, d), lambda i: (0, 0)),
                pl.BlockSpec((1, 1, d), lambda i: (i * tm // rows, 0, 0)),
                pl.BlockSpec((1, 1, d), lambda i: (i * tm // rows, 0, 0))]
    args = [x, g.reshape(1, d), shift, scale]
    if router is None:
        return pl.pallas_call(
            _norm_mod_kernel, out_shape=jax.ShapeDtypeStruct((m, d), BF16), grid=(m // tm,),
            in_specs=in_specs, out_specs=pl.BlockSpec((tm, d), lambda i: (i, 0)),
            compiler_params=_params("parallel"), name="norm_mod")(*args)
    rpad = jnp.zeros((d, LANES), F32).at[:, :router.shape[1]].set(router)
    return pl.pallas_call(
        _norm_mod_route_kernel,
        out_shape=(jax.ShapeDtypeStruct((m, d), BF16), jax.ShapeDtypeStruct((m, LANES), F32)),
        grid=(m // tm,),
        in_specs=in_specs + [pl.BlockSpec((d, LANES), lambda i: (0, 0))],
        out_specs=(pl.BlockSpec((tm, d), lambda i: (i, 0)), pl.BlockSpec((tm, LANES), lambda i: (i, 0))),
        compiler_params=_params("parallel"), name="norm_mod_route")(*args, rpad)


def _mm_kernel(x_ref, w_ref, o_ref):
    o_ref[...] = _dot(x_ref[...], w_ref[...]).astype(o_ref.dtype)


def _mm_res_kernel(x_ref, w_ref, res_ref, gate_ref, o_ref):
    o_ref[...] = res_ref[...] + gate_ref[0] * _dot(x_ref[...], w_ref[...])


def _matmul(x, w, out_dtype=BF16, tm=1024, tn=1024):
    m, k = x.shape
    n = w.shape[1]
    tm, tn = min(tm, m), min(tn, n)
    assert m % tm == 0 and n % tn == 0
    return pl.pallas_call(
        _mm_kernel, out_shape=jax.ShapeDtypeStruct((m, n), out_dtype), grid=(m // tm, n // tn),
        in_specs=[pl.BlockSpec((tm, k), lambda i, j: (i, 0)), pl.BlockSpec((k, tn), lambda i, j: (0, j))],
        out_specs=pl.BlockSpec((tm, tn), lambda i, j: (i, j)),
        compiler_params=_params("parallel", "arbitrary"), name="matmul")(x, w)


def _matmul_residual(x, w, res, gate, tm=1024, tn=1024):
    m, k = x.shape
    n = w.shape[1]
    rows = m // gate.shape[0]
    tm, tn = min(tm, rows), min(tn, n)
    assert rows % tm == 0 and n % tn == 0
    return pl.pallas_call(
        _mm_res_kernel, out_shape=jax.ShapeDtypeStruct((m, n), F32), grid=(m // tm, n // tn),
        in_specs=[pl.BlockSpec((tm, k), lambda i, j: (i, 0)), pl.BlockSpec((k, tn), lambda i, j: (0, j)),
                  pl.BlockSpec((tm, tn), lambda i, j: (i, j)),
                  pl.BlockSpec((1, 1, tn), lambda i, j: (i * tm // rows, 0, j))],
        out_specs=pl.BlockSpec((tm, tn), lambda i, j: (i, j)),
        compiler_params=_params("parallel", "arbitrary"), name="matmul_residual")(x, w, res, gate)


def _ffn_kernel(*refs, gated):
    if gated:
        h_ref, wa_ref, wb_ref, wo_ref, res_ref, mod_ref, gate_ref, o_ref, acc_ref = refs
    else:
        h_ref, wa_ref, wb_ref, wo_ref, res_ref, mod_ref, o_ref, acc_ref = refs
    e = pl.program_id(1)
    h = h_ref[...]
    a = _dot(h, wa_ref[0])
    b = _dot(h, wb_ref[0])
    y = _dot((_silu(a) * b).astype(BF16), wo_ref[0])
    if gated:
        g = gate_ref[...]
        lane = lax.broadcasted_iota(jnp.int32, g.shape, 1)
        y = y * jnp.sum(jnp.where(lane == e, g, 0.0), axis=-1, keepdims=True)

    @pl.when(e == 0)
    def _():
        acc_ref[...] = y

    @pl.when(e > 0)
    def _():
        acc_ref[...] += y

    @pl.when(e == pl.num_programs(1) - 1)
    def _():
        o_ref[...] = res_ref[...] + mod_ref[0] * acc_ref[...]


def _ffn(h, w_in, w_out, res, mod, gates=None, dense_parts=0, tm=512):
    m, d = h.shape
    rows = m // mod.shape[0]
    tm = min(tm, rows)
    assert rows % tm == 0
    if gates is None:
        n_e, f = dense_parts, w_out.shape[1]
        a_map, b_map = (lambda i, e: (0, 0, e)), (lambda i, e: (0, 0, n_e + e))
    else:
        n_e, f = w_in.shape[0], w_out.shape[1]
        a_map, b_map = (lambda i, e: (e, 0, 0)), (lambda i, e: (e, 0, 1))
    in_specs = [pl.BlockSpec((tm, d), lambda i, e: (i, 0)),
                pl.BlockSpec((1, d, f), a_map),
                pl.BlockSpec((1, d, f), b_map),
                pl.BlockSpec((1, f, d), lambda i, e: (e, 0, 0)),
                pl.BlockSpec((tm, d), lambda i, e: (i, 0)),
                pl.BlockSpec((1, 1, d), lambda i, e: (i * tm // rows, 0, 0))]
    args = [h, w_in, w_in, w_out, res, mod]
    if gates is not None:
        in_specs.append(pl.BlockSpec((tm, LANES), lambda i, e: (i, 0)))
        args.append(gates)
    return pl.pallas_call(
        functools.partial(_ffn_kernel, gated=gates is not None),
        out_shape=jax.ShapeDtypeStruct((m, d), F32), grid=(m // tm, n_e),
        in_specs=in_specs, out_specs=pl.BlockSpec((tm, d), lambda i, e: (i, 0)),
        scratch_shapes=[pltpu.VMEM((tm, d), F32)],
        compiler_params=_params("parallel", "arbitrary"), name="ffn")(*args)


def _block_diag_ones(d, group):
    idx = np.arange(d) // group
    return jnp.asarray((idx[:, None] == idx[None, :]).astype(np.float32)).astype(BF16)


def _head_norm_kernel(*refs, group, rope):
    if rope:
        x_ref, gain_ref, bd_ref, cos_ref, sa_ref, sb_ref, o_ref = refs
    else:
        x_ref, gain_ref, bd_ref, o_ref = refs
    x = x_ref[...].astype(F32)
    ss = _dot((x * x).astype(BF16), bd_ref[...])
    y = x * lax.rsqrt(ss * (1.0 / group) + RMS_EPS) * gain_ref[...]
    if rope:
        d = y.shape[-1]
        rep = d // LANES
        half = group // 4
        y = (y * pltpu.repeat(cos_ref[...], rep, axis=1)
             + pltpu.roll(y, d - half, 1) * pltpu.repeat(sa_ref[...], rep, axis=1)
             + pltpu.roll(y, half, 1) * pltpu.repeat(sb_ref[...], rep, axis=1))
    o_ref[...] = y.astype(o_ref.dtype)


def _rope_tables(seq):
    pos = np.arange(seq)
    lane = np.arange(LANES) % DIFF_HEAD_DIM
    half = DIFF_HEAD_DIM // 2
    j = lane % half
    inv = ROPE_BASE ** (-(2.0 * (j % (half // 2))) / half)
    p = np.where(lane[None, :] < half, (pos // GRID_W)[:, None], (pos % GRID_W)[:, None])
    ang = p * inv[None, :]
    first = (j < half // 2)[None, :]
    cos, sin = np.cos(ang), np.sin(ang)
    return (jnp.asarray(cos, F32), jnp.asarray(np.where(first, -sin, 0.0), F32),
            jnp.asarray(np.where(first, 0.0, sin), F32))


def _head_norm(src, col_block, gain, group, seq=None, rope=False, tm=512):
    m = src.shape[0]
    d = D_MODEL
    tm = min(tm, m if seq is None else seq)
    assert m % tm == 0
    in_specs = [pl.BlockSpec((tm, d), lambda i: (i, col_block)),
                pl.BlockSpec((1, d), lambda i: (0, 0)),
                pl.BlockSpec((d, d), lambda i: (0, 0))]
    args = [src, gain.reshape(1, d), _block_diag_ones(d, group)]
    if rope:
        nblk = seq // tm
        in_specs += [pl.BlockSpec((tm, LANES), lambda i: (i % nblk, 0))] * 3
        args += list(_rope_tables(seq))
    return pl.pallas_call(
        functools.partial(_head_norm_kernel, group=group, rope=rope),
        out_shape=jax.ShapeDtypeStruct((m, d), BF16), grid=(m // tm,),
        in_specs=in_specs, out_specs=pl.BlockSpec((tm, d), lambda i: (i, 0)),
        compiler_params=_params("parallel"), name="head_norm")(*args)


def _na_bias_table(rpb, rows):
    wr = min(NA_WIN_ROWS, rows)
    cols = np.arange(GRID_W)
    c0 = np.clip(cols - NA_WIN_COLS // 2, 0, GRID_W - NA_WIN_COLS)
    col_valid = (cols[None, :] >= c0[:, None]) & (cols[None, :] < c0[:, None] + NA_WIN_COLS)
    col_idx = np.clip(cols[None, :] - cols[:, None], 1 - NA_WIN_COLS, NA_WIN_COLS - 1) + NA_WIN_COLS - 1
    variants = []
    for off in range(wr):
        row_idx = np.arange(wr) - off + NA_WIN_ROWS - 1
        bias = jnp.transpose(rpb[:, row_idx][:, :, col_idx], (0, 2, 1, 3))
        bias = jnp.where(col_valid[None, :, None, :], bias, NEG_BIG)
        variants.append(bias.reshape(NA_HEADS, GRID_W, wr * GRID_W))
    return jnp.stack(variants).astype(F32)


def _na_row_start(r, rows, wr):
    return jnp.clip(r - wr // 2, 0, rows - wr)


def _softmax_pv(s_loc, s_ctx, v_loc, v_ctx):
    m = jnp.maximum(jnp.max(s_loc, axis=-1, keepdims=True), jnp.max(s_ctx, axis=-1, keepdims=True))
    p_loc = jnp.exp(s_loc - m)
    p_ctx = jnp.exp(s_ctx - m)
    z = jnp.sum(p_loc, axis=-1, keepdims=True) + jnp.sum(p_ctx, axis=-1, keepdims=True)
    o = _dot(p_loc.astype(BF16), v_loc) + _dot(p_ctx.astype(BF16), v_ctx)
    return o * (1.0 / z)


def _na_kernel(q_ref, k_ref, v_ref, kc_ref, vc_ref, bias_ref, o_ref, *, rows, wr):
    r = pl.program_id(1)
    start = pl.multiple_of(_na_row_start(r, rows, wr) * GRID_W, GRID_W)
    lane = lax.broadcasted_iota(jnp.int32, (GRID_W, LANES), 1)
    low = lane < NA_HEAD_DIM
    for pair in range(NA_HEADS // 2):
        cs = slice(pair * LANES, (pair + 1) * LANES)
        q = q_ref[0, :, cs]
        k = k_ref[0, pl.ds(start, wr * GRID_W), cs]
        v = v_ref[0, pl.ds(start, wr * GRID_W), cs]
        kc = kc_ref[0, :, cs]
        vc = vc_ref[0, :, cs]
        outs = []
        for sub in range(2):
            qm = jnp.where(low if sub == 0 else ~low, q, jnp.zeros_like(q))
            s_loc = _dot_t(qm, k) + bias_ref[0, 2 * pair + sub]
            s_ctx = _dot_t(qm, kc)
            outs.append(_softmax_pv(s_loc, s_ctx, v, vc))
        o_ref[0, :, cs] = jnp.where(low, outs[0], outs[1]).astype(o_ref.dtype)


def _na_ctx_kernel(q_ref, k_ref, v_ref, o_ref):
    lane = lax.broadcasted_iota(jnp.int32, (q_ref.shape[1], LANES), 1)
    low = lane < NA_HEAD_DIM
    for pair in range(NA_HEADS // 2):
        cs = slice(pair * LANES, (pair + 1) * LANES)
        q, k, v = q_ref[0, :, cs], k_ref[0, :, cs], v_ref[0, :, cs]
        outs = []
        for sub in range(2):
            qm = jnp.where(low if sub == 0 else ~low, q, jnp.zeros_like(q))
            s = _dot_t(qm, k)
            p = jnp.exp(s - jnp.max(s, axis=-1, keepdims=True))
            outs.append(_dot(p.astype(BF16), v) * (1.0 / jnp.sum(p, axis=-1, keepdims=True)))
        o_ref[0, :, cs] = jnp.where(low, outs[0], outs[1]).astype(o_ref.dtype)


def _neighbourhood_attention(q, k, qkv, qc, kc, qkv_c, rpb, batch):
    d = D_MODEL
    s = q.shape[0] // batch
    n_ctx = qc.shape[0] // batch
    rows = s // GRID_W
    wr = min(NA_WIN_ROWS, rows)
    bias = _na_bias_table(rpb, rows)
    q3, k3, qkv3 = q.reshape(batch, s, d), k.reshape(batch, s, d), qkv.reshape(batch, s, 3 * d)
    qc3, kc3, qkvc3 = qc.reshape(batch, n_ctx, d), kc.reshape(batch, n_ctx, d), qkv_c.reshape(batch, n_ctx, 3 * d)

    def variant(b, r):
        return (r - _na_row_start(r, rows, wr), 0, 0, 0)

    ol = pl.pallas_call(
        functools.partial(_na_kernel, rows=rows, wr=wr),
        out_shape=jax.ShapeDtypeStruct((batch, s, d), BF16), grid=(batch, rows),
        in_specs=[pl.BlockSpec((1, GRID_W, d), lambda b, r: (b, r, 0)),
                  pl.BlockSpec((1, s, d), lambda b, r: (b, 0, 0)),
                  pl.BlockSpec((1, s, d), lambda b, r: (b, 0, 2)),
                  pl.BlockSpec((1, n_ctx, d), lambda b, r: (b, 0, 0)),
                  pl.BlockSpec((1, n_ctx, d), lambda b, r: (b, 0, 2)),
                  pl.BlockSpec((1, NA_HEADS, GRID_W, wr * GRID_W), variant)],
        out_specs=pl.BlockSpec((1, GRID_W, d), lambda b, r: (b, r, 0)),
        compiler_params=_params("parallel", "arbitrary"), name="na_attention",
    )(q3, k3, qkv3, kc3, qkvc3, bias)
    oc = pl.pallas_call(
        _na_ctx_kernel, out_shape=jax.ShapeDtypeStruct((batch, n_ctx, d), BF16), grid=(batch,),
        in_specs=[pl.BlockSpec((1, n_ctx, d), lambda b: (b, 0, 0)),
                  pl.BlockSpec((1, n_ctx, d), lambda b: (b, 0, 0)),
                  pl.BlockSpec((1, n_ctx, d), lambda b: (b, 0, 2))],
        out_specs=pl.BlockSpec((1, n_ctx, d), lambda b: (b, 0, 0)),
        compiler_params=_params("parallel"), name="na_ctx_attention",
    )(qc3, kc3, qkvc3)
    return ol.reshape(batch * s, d), oc.reshape(batch * n_ctx, d)


def _diff_lambda(lam_ref, lambda_init):
    lv = lam_ref[...]
    a = jnp.sum(lv[0:1] * lv[1:2], axis=-1, keepdims=True)
    b = jnp.sum(lv[2:3] * lv[3:4], axis=-1, keepdims=True)
    return jnp.exp(a) - jnp.exp(b) + lambda_init


def _diff_out(o, onorm_ref, lambda_init, o_ref):
    y = o * lax.rsqrt(jnp.mean(o * o, axis=-1, keepdims=True) + RMS_EPS) * onorm_ref[...]
    o_ref[0] = (y * (1.0 - lambda_init)).astype(o_ref.dtype)


def _diff_kernel(q_ref, k_ref, v_ref, kc_ref, vc_ref, lam_ref, onorm_ref, o_ref, *, lambda_init):
    lam = _diff_lambda(lam_ref, lambda_init)
    q = q_ref[0]
    k, v, kc, vc = k_ref[0], v_ref[0], kc_ref[0], vc_ref[0]
    low = lax.broadcasted_iota(jnp.int32, q.shape, 1) < DIFF_HEAD_DIM
    a_loc = a_ctx = None
    for sub in range(2):
        qm = jnp.where(low if sub == 0 else ~low, q, jnp.zeros_like(q))
        s_loc = _dot_t(qm, k)
        s_ctx = _dot_t(qm, kc)
        m = jnp.maximum(jnp.max(s_loc, axis=-1, keepdims=True), jnp.max(s_ctx, axis=-1, keepdims=True))
        p_loc = jnp.exp(s_loc - m)
        p_ctx = jnp.exp(s_ctx - m)
        z = jnp.sum(p_loc, axis=-1, keepdims=True) + jnp.sum(p_ctx, axis=-1, keepdims=True)
        w = (1.0 / z) if sub == 0 else (-lam / z)
        a_loc = p_loc * w if sub == 0 else a_loc + p_loc * w
        a_ctx = p_ctx * w if sub == 0 else a_ctx + p_ctx * w
    o = _dot(a_loc.astype(BF16), v) + _dot(a_ctx.astype(BF16), vc)
    _diff_out(o, onorm_ref, lambda_init, o_ref)


def _diff_ctx_kernel(q_ref, k_ref, v_ref, lam_ref, onorm_ref, o_ref, *, lambda_init):
    lam = _diff_lambda(lam_ref, lambda_init)
    q, k, v = q_ref[0], k_ref[0], v_ref[0]
    low = lax.broadcasted_iota(jnp.int32, q.shape, 1) < DIFF_HEAD_DIM
    a = None
    for sub in range(2):
        qm = jnp.where(low if sub == 0 else ~low, q, jnp.zeros_like(q))
        s = _dot_t(qm, k)
        p = jnp.exp(s - jnp.max(s, axis=-1, keepdims=True))
        z = jnp.sum(p, axis=-1, keepdims=True)
        a = p * (1.0 / z) if sub == 0 else a - p * (lam / z)
    _diff_out(_dot(a.astype(BF16), v), onorm_ref, lambda_init, o_ref)


def _diff_attention(q, k, qkv, qc, kc, qkv_c, lam_vecs, out_norm, lambda_init, batch, tq=256):
    d = D_MODEL
    hw = 2 * DIFF_HEAD_DIM
    s = q.shape[0] // batch
    n_ctx = qc.shape[0] // batch
    v_blk = 2 * d // hw
    q3, k3, qkv3 = q.reshape(batch, s, d), k.reshape(batch, s, d), qkv.reshape(batch, s, 3 * d)
    qc3, kc3, qkvc3 = qc.reshape(batch, n_ctx, d), kc.reshape(batch, n_ctx, d), qkv_c.reshape(batch, n_ctx, 3 * d)
    onorm = out_norm.reshape(1, hw)
    ol = pl.pallas_call(
        functools.partial(_diff_kernel, lambda_init=lambda_init),
        out_shape=jax.ShapeDtypeStruct((batch, s, d), BF16), grid=(batch, DIFF_HEADS, s // tq),
        in_specs=[pl.BlockSpec((1, tq, hw), lambda b, h, i: (b, i, h)),
                  pl.BlockSpec((1, s, hw), lambda b, h, i: (b, 0, h)),
                  pl.BlockSpec((1, s, hw), lambda b, h, i: (b, 0, v_blk + h)),
                  pl.BlockSpec((1, n_ctx, hw), lambda b, h, i: (b, 0, h)),
                  pl.BlockSpec((1, n_ctx, hw), lambda b, h, i: (b, 0, v_blk + h)),
                  pl.BlockSpec((4, DIFF_HEAD_DIM), lambda b, h, i: (0, 0)),
                  pl.BlockSpec((1, hw), lambda b, h, i: (0, 0))],
        out_specs=pl.BlockSpec((1, tq, hw), lambda b, h, i: (b, i, h)),
        compiler_params=_params("parallel", "arbitrary", "arbitrary"), name="diff_attention",
    )(q3, k3, qkv3, kc3, qkvc3, lam_vecs, onorm)
    oc = pl.pallas_call(
        functools.partial(_diff_ctx_kernel, lambda_init=lambda_init),
        out_shape=jax.ShapeDtypeStruct((batch, n_ctx, d), BF16), grid=(batch, DIFF_HEADS),
        in_specs=[pl.BlockSpec((1, n_ctx, hw), lambda b, h: (b, 0, h)),
                  pl.BlockSpec((1, n_ctx, hw), lambda b, h: (b, 0, h)),
                  pl.BlockSpec((1, n_ctx, hw), lambda b, h: (b, 0, v_blk + h)),
                  pl.BlockSpec((4, DIFF_HEAD_DIM), lambda b, h: (0, 0)),
                  pl.BlockSpec((1, hw), lambda b, h: (0, 0))],
        out_specs=pl.BlockSpec((1, n_ctx, hw), lambda b, h: (b, 0, h)),
        compiler_params=_params("parallel", "arbitrary"), name="diff_ctx_attention",
    )(qc3, kc3, qkvc3, lam_vecs, onorm)
    return ol.reshape(batch * s, d), oc.reshape(batch * n_ctx, d)


def _conv_kernel(x_ref, w_ref, o_ref, *, act, n_norm_q, n_norm, qscale):
    x = x_ref[0].astype(F32)
    seq = x.shape[0]
    row = lax.broadcasted_iota(jnp.int32, x.shape, 0)
    prev = jnp.where(row == 0, 0.0, pltpu.roll(x, 1, 0))
    nxt = jnp.where(row == seq - 1, 0.0, pltpu.roll(x, seq - 1, 0))
    w = w_ref[...]
    y = prev * w[0:1] + x * w[1:2] + nxt * w[2:3]
    if act:
        y = _silu(y)
    if n_norm:
        j = pl.program_id(1)
        parts = []
        for g in range(y.shape[1] // LANES):
            seg = y[:, g * LANES:(g + 1) * LANES]
            inv = lax.rsqrt(jnp.sum(seg * seg, axis=-1, keepdims=True) + L2_EPS)
            scale = jnp.where(j < n_norm_q, inv * qscale, jnp.where(j < n_norm, inv, 1.0))
            parts.append(seg * scale)
        y = jnp.concatenate(parts, axis=-1) if len(parts) > 1 else parts[0]
    o_ref[0] = y.astype(o_ref.dtype)


def _dwconv3(x, w, batch, act=False, l2norm_cols=0, qscale=1.0, tn=256):
    m, c = x.shape
    seq = m // batch
    n_norm = l2norm_cols // tn
    out = pl.pallas_call(
        functools.partial(_conv_kernel, act=act, n_norm_q=n_norm // 2, n_norm=n_norm, qscale=qscale),
        out_shape=jax.ShapeDtypeStruct((batch, seq, c), BF16), grid=(batch, c // tn),
        in_specs=[pl.BlockSpec((1, seq, tn), lambda b, j: (b, 0, j)),
                  pl.BlockSpec((3, tn), lambda b, j: (0, j))],
        out_specs=pl.BlockSpec((1, seq, tn), lambda b, j: (b, 0, j)),
        compiler_params=_params("parallel", "arbitrary"), name="dwconv3",
    )(x.reshape(batch, seq, c), w)
    return out.reshape(m, c)


def _gdn_gate_kernel(h_ref, w_ref, wt_ref, a_ref, at_ref, bias_ref, biast_ref, col_ref, row_ref):
    h = h_ref[...]
    col = _dot(h, w_ref[...])
    row = _dot_t(wt_ref[...], h)

    def act(z, neg_a, bias, is_g):
        zb = z + bias
        softplus = jnp.maximum(zb, 0.0) + jnp.log(1.0 + jnp.exp(-jnp.abs(zb)))
        return jnp.where(is_g, neg_a * softplus, 1.0 / (1.0 + jnp.exp(-z)))

    lane = lax.broadcasted_iota(jnp.int32, col.shape, 1)
    col_ref[...] = act(col, a_ref[...], bias_ref[...], lane < 2 * GDN_HEADS)
    sub = lax.broadcasted_iota(jnp.int32, row.shape, 0)
    row_ref[...] = act(row, at_ref[...], biast_ref[...], sub < 2 * GDN_HEADS)


def _gdn_gates(h, w_beta, w_decay, a_log, dt_bias, tm=512):
    m, d = h.shape
    nh = GDN_HEADS
    tm = min(tm, m)
    w = jnp.concatenate([w_decay[0], w_decay[1], w_beta[0], w_beta[1]], axis=-1)
    wpad = jnp.zeros((d, LANES), F32).at[:, :4 * nh].set(w).astype(BF16)
    wt = w.T.astype(BF16)
    neg_a = jnp.concatenate([-jnp.exp(a_log[0]), -jnp.exp(a_log[1]), jnp.zeros((2 * nh,), F32)])
    bias = jnp.concatenate([dt_bias[0], dt_bias[1], jnp.zeros((2 * nh,), F32)])
    pad = lambda v: jnp.zeros((1, LANES), F32).at[0, :4 * nh].set(v)
    return pl.pallas_call(
        _gdn_gate_kernel,
        out_shape=(jax.ShapeDtypeStruct((m, LANES), F32), jax.ShapeDtypeStruct((4 * nh, m), F32)),
        grid=(m // tm,),
        in_specs=[pl.BlockSpec((tm, d), lambda i: (i, 0)),
                  pl.BlockSpec((d, LANES), lambda i: (0, 0)),
                  pl.BlockSpec((4 * nh, d), lambda i: (0, 0)),
                  pl.BlockSpec((1, LANES), lambda i: (0, 0)),
                  pl.BlockSpec((4 * nh, 1), lambda i: (0, 0)),
                  pl.BlockSpec((1, LANES), lambda i: (0, 0)),
                  pl.BlockSpec((4 * nh, 1), lambda i: (0, 0))],
        out_specs=(pl.BlockSpec((tm, LANES), lambda i: (i, 0)), pl.BlockSpec((4 * nh, tm), lambda i: (0, i))),
        compiler_params=_params("parallel"), name="gdn_gates",
    )(h, wpad, wt, pad(neg_a), neg_a.reshape(4 * nh, 1), pad(bias), bias.reshape(4 * nh, 1))


def _gdn_local_kernel(q_ref, k_ref, v_ref, col_ref, row_ref,
                      u_ref, w_ref, qg_ref, kd_ref, att_ref, dl_ref):
    c = GDN_CHUNK
    nh = GDN_HEADS
    hd = GDN_HEAD_DIM
    ii = lax.broadcasted_iota(jnp.int32, (c, c), 0)
    jj = lax.broadcasted_iota(jnp.int32, (c, c), 1)
    eye = (ii == jj).astype(F32)
    incl = [ii >= jj, ii <= jj]
    strict = [ii > jj, ii < jj]
    col = col_ref[0]
    row = row_ref[0, 0]
    gc_col, gc_row = [], []
    for d in range(2):
        m_col = incl[d].astype(F32)
        m_row = incl[1 - d].astype(F32)
        gc_col.append(jnp.dot(m_col, col[:, d * nh:(d + 1) * nh], preferred_element_type=F32, precision=HIGHEST))
        gc_row.append(jnp.dot(row[d * nh:(d + 1) * nh], m_row, preferred_element_type=F32, precision=HIGHEST))
    for h in range(nh):
        cs = slice(h * hd, (h + 1) * hd)
        q, k, v = q_ref[0, :, cs], k_ref[0, :, cs], v_ref[0, :, cs]
        kf, vf = k.astype(F32), v.astype(F32)
        kk = _dot_t(k, k)
        qk = _dot_t(q, k)
        for d in range(2):
            gcc = gc_col[d][:, h:h + 1]
            gcr = gc_row[d][h:h + 1, :]
            beta = col[:, 2 * nh + d * nh + h:2 * nh + d * nh + h + 1]
            dec = jnp.exp(jnp.where(incl[d], gcc - gcr, NEG_BIG))
            neg_a = jnp.where(strict[d], -(beta * kk * dec), 0.0)
            p = eye + neg_a
            mpow = neg_a
            for _ in range(5):
                mb = mpow.astype(BF16)
                mpow = _dot(mb, mb)
                p = p + _dot(p.astype(BF16), mpow.astype(BF16))
            eg = jnp.exp(gcc)
            rhs = jnp.concatenate([vf * beta, kf * (beta * eg)], axis=-1).astype(BF16)
            sol = _dot(p.astype(BF16), rhs)
            g_last = gcr[:, c - 1:c] if d == 0 else gcr[:, 0:1]
            u_ref[d, 0, :, cs] = sol[:, :hd]
            w_ref[d, 0, :, cs] = sol[:, hd:].astype(w_ref.dtype)
            qg_ref[d, 0, :, cs] = (q.astype(F32) * eg).astype(qg_ref.dtype)
            kd_ref[d, 0, :, cs] = (kf * jnp.exp(g_last - gcc)).astype(kd_ref.dtype)
            att_ref[d, 0, 0, h] = (qk * dec).astype(att_ref.dtype)
            dl_ref[d, 0, 0, h:h + 1, :] = jnp.broadcast_to(jnp.exp(g_last), (1, hd))


def _gdn_scan_kernel(uf_ref, wf_ref, qgf_ref, kdf_ref, attf_ref, dlf_ref,
                     ub_ref, wb_ref, qgb_ref, kdb_ref, attb_ref, dlb_ref, s0_ref,
                     of_ref, ob_ref, s_ref):
    hd = GDN_HEAD_DIM

    @pl.when(pl.program_id(1) == 0)
    def _():
        s_ref[...] = s0_ref[...]

    dirs = ((uf_ref, wf_ref, qgf_ref, kdf_ref, attf_ref, dlf_ref, of_ref),
            (ub_ref, wb_ref, qgb_ref, kdb_ref, attb_ref, dlb_ref, ob_ref))
    for d, (u_ref, w_ref, qg_ref, kd_ref, att_ref, dl_ref, o_ref) in enumerate(dirs):
        for h in range(GDN_HEADS):
            cs = slice(h * hd, (h + 1) * hd)
            s = s_ref[0, d, h]
            sb = s.astype(BF16)
            v_new = u_ref[0, 0, :, cs] - _dot(w_ref[0, 0, :, cs], sb)
            vb = v_new.astype(BF16)
            o_ref[0, :, cs] = _dot(qg_ref[0, 0, :, cs], sb) + _dot(att_ref[0, 0, 0, h], vb)
            upd = lax.dot_general(kd_ref[0, 0, :, cs], vb, (((0,), (0,)), ((), ())), preferred_element_type=F32)
            s_ref[0, d, h] = s * dl_ref[0, 0, 0, h:h + 1, :] + upd


def _gdn_core(u3, gcol, grow, s0, batch):
    c = GDN_CHUNK
    nh, hd = GDN_HEADS, GDN_HEAD_DIM
    d = nh * hd
    m = u3.shape[0]
    seq = m // batch
    n = seq // c
    u33 = u3.reshape(batch, seq, 3 * d)
    gcol3 = gcol.reshape(batch, seq, LANES)
    grow4 = jnp.transpose(grow.reshape(4 * nh, batch, n, c), (1, 2, 0, 3))
    big = lambda dt: jax.ShapeDtypeStruct((2, batch, seq, d), dt)
    blk = pl.BlockSpec((2, 1, c, d), lambda b, i: (0, b, i, 0))
    u, w, qg, kd, att, dl = pl.pallas_call(
        _gdn_local_kernel,
        out_shape=(big(F32), big(BF16), big(BF16), big(BF16),
                   jax.ShapeDtypeStruct((2, batch, n, nh, c, c), BF16),
                   jax.ShapeDtypeStruct((2, batch, n, nh, hd), F32)),
        grid=(batch, n),
        in_specs=[pl.BlockSpec((1, c, d), lambda b, i: (b, i, 0)),
                  pl.BlockSpec((1, c, d), lambda b, i: (b, i, 1)),
                  pl.BlockSpec((1, c, d), lambda b, i: (b, i, 2)),
                  pl.BlockSpec((1, c, LANES), lambda b, i: (b, i, 0)),
                  pl.BlockSpec((1, 1, 4 * nh, c), lambda b, i: (b, i, 0, 0))],
        out_specs=(blk, blk, blk, blk,
                   pl.BlockSpec((2, 1, 1, nh, c, c), lambda b, i: (0, b, i, 0, 0, 0)),
                   pl.BlockSpec((2, 1, 1, nh, hd), lambda b, i: (0, b, i, 0, 0))),
        compiler_params=_params("parallel", "arbitrary"), name="gdn_local",
    )(u33, u33, u33, gcol3, grow4)

    def dir_specs(dd):
        pos = (lambda i: i) if dd == 0 else (lambda i: n - 1 - i)
        big_blk = pl.BlockSpec((1, 1, c, d), lambda b, i: (dd, b, pos(i), 0))
        return [big_blk, big_blk, big_blk, big_blk,
                pl.BlockSpec((1, 1, 1, nh, c, c), lambda b, i: (dd, b, pos(i), 0, 0, 0)),
                pl.BlockSpec((1, 1, 1, nh, hd), lambda b, i: (dd, b, pos(i), 0, 0))]

    s_blk = pl.BlockSpec((1, 2, nh, hd, hd), lambda b, i: (b, 0, 0, 0, 0))
    o_f, o_b, s_fin = pl.pallas_call(
        _gdn_scan_kernel,
        out_shape=(jax.ShapeDtypeStruct((batch, seq, d), F32), jax.ShapeDtypeStruct((batch, seq, d), F32),
                   jax.ShapeDtypeStruct((batch, 2, nh, hd, hd), F32)),
        grid=(batch, n),
        in_specs=dir_specs(0) + dir_specs(1) + [s_blk],
        out_specs=(pl.BlockSpec((1, c, d), lambda b, i: (b, i, 0)),
                   pl.BlockSpec((1, c, d), lambda b, i: (b, n - 1 - i, 0)), s_blk),
        compiler_params=_params("parallel", "arbitrary"), name="gdn_scan",
    )(u, w, qg, kd, att, dl, u, w, qg, kd, att, dl, s0)
    return o_f.reshape(m, d), o_b.reshape(m, d), s_fin


def _gdn_out_kernel(of_ref, ob_ref, gate_ref, norm_ref, o_ref):
    o = of_ref[...] + ob_ref[...]
    gate = _silu(gate_ref[...].astype(F32))
    parts = []
    for h in range(GDN_HEADS):
        seg = o[:, h * GDN_HEAD_DIM:(h + 1) * GDN_HEAD_DIM]
        parts.append(seg * lax.rsqrt(jnp.mean(seg * seg, axis=-1, keepdims=True) + RMS_EPS) * norm_ref[...])
    o_ref[...] = (jnp.concatenate(parts, axis=-1) * gate).astype(o_ref.dtype)


def _gdn_out(o_f, o_b, gate_lin, out_norm, tm=512):
    m, d = o_f.shape
    tm = min(tm, m)
    blk = pl.BlockSpec((tm, d), lambda i: (i, 0))
    return pl.pallas_call(
        _gdn_out_kernel, out_shape=jax.ShapeDtypeStruct((m, d), BF16), grid=(m // tm,),
        in_specs=[blk, blk, blk, pl.BlockSpec((1, GDN_HEAD_DIM), lambda i: (0, 0))],
        out_specs=blk, compiler_params=_params("parallel"), name="gdn_out",
    )(o_f, o_b, gate_lin, out_norm.reshape(1, GDN_HEAD_DIM))


def _gdn_branch(h, batch, s0, p):
    d = GDN_HEADS * GDN_HEAD_DIM
    lin = _matmul(h, p['w_qkv'])
    u3 = _dwconv3(lin, p['conv'], batch, act=True, l2norm_cols=2 * d, qscale=GDN_HEAD_DIM ** -0.5)
    gcol, grow = _gdn_gates(h, p['w_beta'], p['w_decay'], p['a_log'], p['dt_bias'])
    o_f, o_b, s_fin = _gdn_core(u3, gcol, grow, s0, batch)
    gate_lin = _matmul(h, p['w_gate'])
    return _gdn_out(o_f, o_b, gate_lin, p['out_norm']), s_fin


def _hy_filter_kernel(z_ref, w1_ref, b1_ref, fr_ref, w2_ref, b2_ref, w3f_ref, b3f_ref, w3b_ref, b3b_ref,
                      dl_ref, hf_ref, hb_ref):
    hdot = functools.partial(jnp.dot, preferred_element_type=F32, precision=HIGHEST)
    z = z_ref[...]
    fr = fr_ref[...]
    h1 = jnp.sin(fr[0:1] * (hdot(z, w1_ref[...]) + b1_ref[...]))
    h2 = jnp.sin(fr[1:2] * (hdot(h1, w2_ref[...]) + b2_ref[...]))
    decay = jnp.exp(-z[:, 0:1] * dl_ref[...])
    hf = (hdot(h2, w3f_ref[...]) + b3f_ref[...]) * decay
    hb = (hdot(h2, w3b_ref[...]) + b3b_ref[...]) * decay
    inv = 1.0 / (jnp.sum(jnp.abs(hf), axis=0, keepdims=True) + jnp.sum(jnp.abs(hb), axis=0, keepdims=True))
    hf_ref[0] = hf * inv
    hb_ref[0] = hb * inv


def _hyena_filters(seq, p, tn=256):
    d = D_MODEL
    width = p['w2'].shape[0]
    t = np.linspace(0.0, 1.0, seq)[:, None]
    bands = (HY_EMB_DIM - 1) // 2
    ang = (2.0 * math.pi * np.arange(seq) / seq)[:, None] * np.linspace(1e-4, bands - 1, bands)[None, :]
    feats = np.zeros((seq, LANES), np.float32)
    feats[:, :HY_EMB_DIM] = np.concatenate([t, np.cos(ang), -np.sin(ang)], axis=-1)
    w1 = jnp.zeros((LANES, width), F32).at[:HY_EMB_DIM].set(p['w1'])
    deltas = np.abs(np.linspace(HY_MIN_DECAY, HY_MAX_DECAY, d)).astype(np.float32)[None, :]
    nj = d // tn
    full = lambda shape: pl.BlockSpec(shape, lambda n, j: (0,) * len(shape))
    w3 = lambda dd: pl.BlockSpec((width, tn), lambda n, j: (0, (2 * n + dd) * nj + j))
    b3 = lambda dd: pl.BlockSpec((1, tn), lambda n, j: (0, (2 * n + dd) * nj + j))
    out = jax.ShapeDtypeStruct((HY_ORDER, seq, d), F32)
    oblk = pl.BlockSpec((1, seq, tn), lambda n, j: (n, 0, j))
    b3row = p['b3'].reshape(1, -1)
    return pl.pallas_call(
        _hy_filter_kernel, out_shape=(out, out), grid=(HY_ORDER, nj),
        in_specs=[full((seq, LANES)), full((LANES, width)), full((1, width)), full((2, width)),
                  full((width, width)), full((1, width)), w3(0), b3(0), w3(1), b3(1),
                  pl.BlockSpec((1, tn), lambda n, j: (0, j))],
        out_specs=(oblk, oblk), compiler_params=_params("arbitrary", "arbitrary"), name="hyena_filters",
    )(jnp.asarray(feats), w1, p['b1'].reshape(1, width), p['freq'], p['w2'], p['b2'].reshape(1, width),
      p['w3'], b3row, p['w3'], b3row, jnp.asarray(deltas))


def _dft_matrices(seq):
    f = lax.broadcasted_iota(jnp.int32, (seq, seq), 0)
    t = lax.broadcasted_iota(jnp.int32, (seq, seq), 1)
    ang = ((f * t) % (2 * seq)).astype(F32) * (math.pi / seq)
    nyq = (1 - 2 * (t % 2)).astype(F32)
    fwd = jnp.stack([jnp.cos(ang), jnp.where(f == 0, nyq, -jnp.sin(ang))])
    wgt = jnp.where(lax.broadcasted_iota(jnp.int32, (1, 1, seq), 2) == 0, 0.5 / seq, 1.0 / seq)
    inv = jnp.transpose(fwd, (0, 2, 1)) * wgt
    return fwd.astype(BF16), inv.astype(BF16)


def _hy_spectrum_kernel(f_ref, hf_ref, hb_ref, hr_ref, hi_ref, t_ref):
    hf = hf_ref[0]
    row = lax.broadcasted_iota(jnp.int32, hf.shape, 0)
    hb = jnp.where(row == 0, 0.0, hb_ref[0])
    hs = (hf + hb).astype(BF16)
    hd = (hf - hb).astype(BF16)
    hr = _dot(f_ref[0], hs)
    hi = _dot(f_ref[1], hd)
    nyq = _dot(f_ref[1, 0:8, :], hs)[0:1]
    orow = lax.broadcasted_iota(jnp.int32, hr.shape, 0)
    first = jnp.logical_and(pl.program_id(0) == 0, orow == 0)
    hr_ref[0] = hr
    hi_ref[0] = jnp.where(first, 0.0, hi)
    t_ref[0] = jnp.where(first, nyq, hr)


def _hyena_spectrum(hf, hb, fwd, fm=1024, tn=256):
    n_ord, seq, d = hf.shape
    fm = min(fm, seq)
    out = jax.ShapeDtypeStruct((n_ord, seq, d), F32)
    hblk = pl.BlockSpec((1, seq, tn), lambda c, n, j: (n, 0, j))
    oblk = pl.BlockSpec((1, fm, tn), lambda c, n, j: (n, c, j))
    return pl.pallas_call(
        _hy_spectrum_kernel, out_shape=(out, out, out), grid=(seq // fm, n_ord, d // tn),
        in_specs=[pl.BlockSpec((2, fm, seq), lambda c, n, j: (0, c, 0)), hblk, hblk],
        out_specs=(oblk, oblk, oblk),
        compiler_params=_params("arbitrary", "arbitrary", "arbitrary"), name="hyena_spectrum",
    )(fwd, hf, hb)


def _hy_fwd_kernel(f_ref, z_ref, hr_ref, hi_ref, t_ref, y_ref):
    u = z_ref[0]
    xr = _dot(f_ref[0], u)
    xi = _dot(f_ref[1], u)
    hr, hi, tt = hr_ref[0], hi_ref[0], t_ref[0]
    y_ref[0, 0] = (xr * hr - xi * hi).astype(y_ref.dtype)
    y_ref[0, 1] = (xr * hi + xi * tt).astype(y_ref.dtype)


def _hyena_fwd(z3, z_col0, fwd, hr, hi, tt, order, fm=1024, tn=256):
    batch, seq, _ = z3.shape
    d = D_MODEL
    fm = min(fm, seq)
    zoff = z_col0 // tn
    hblk = pl.BlockSpec((1, fm, tn), lambda c, b, j: (order, c, j))
    return pl.pallas_call(
        _hy_fwd_kernel, out_shape=jax.ShapeDtypeStruct((batch, 2, seq, d), BF16),
        grid=(seq // fm, batch, d // tn),
        in_specs=[pl.BlockSpec((2, fm, seq), lambda c, b, j: (0, c, 0)),
                  pl.BlockSpec((1, seq, tn), lambda c, b, j: (b, 0, zoff + j)), hblk, hblk, hblk],
        out_specs=pl.BlockSpec((1, 2, fm, tn), lambda c, b, j: (b, 0, c, j)),
        compiler_params=_params("arbitrary", "arbitrary", "arbitrary"), name="hyena_fwd_dft",
    )(fwd, z3, hr, hi, tt)


def _hy_inv_kernel(g_ref, y_ref, z_ref, gate_ref, skip_ref, o_ref):
    y = _dot(g_ref[0], y_ref[0, 0]) + _dot(g_ref[1], y_ref[0, 1])
    conv = y + z_ref[0].astype(F32) * skip_ref[...]
    o_ref[0] = (gate_ref[0].astype(F32) * conv).astype(o_ref.dtype)


def _hyena_inv(y, inv, z3, z_col0, gate3, gate_col0, skip, tmc=1024, tn=256):
    batch, _, seq, d = y.shape
    tmc = min(tmc, seq)
    zoff, goff = z_col0 // tn, gate_col0 // tn
    return pl.pallas_call(
        _hy_inv_kernel, out_shape=jax.ShapeDtypeStruct((batch, seq, d), BF16),
        grid=(seq // tmc, batch, d // tn),
        in_specs=[pl.BlockSpec((2, tmc, seq), lambda c, b, j: (0, c, 0)),
                  pl.BlockSpec((1, 2, seq, tn), lambda c, b, j: (b, 0, 0, j)),
                  pl.BlockSpec((1, tmc, tn), lambda c, b, j: (b, c, zoff + j)),
                  pl.BlockSpec((1, tmc, tn), lambda c, b, j: (b, c, goff + j)),
                  pl.BlockSpec((1, tn), lambda c, b, j: (0, j))],
        out_specs=pl.BlockSpec((1, tmc, tn), lambda c, b, j: (b, c, j)),
        compiler_params=_params("arbitrary", "arbitrary", "arbitrary"), name="hyena_inv_dft",
    )(inv, y, z3, gate3, skip.reshape(1, d))


def _hyena_branch(h, batch, p):
    d = D_MODEL
    m = h.shape[0]
    seq = m // batch
    xs = _dwconv3(_matmul(h, p['w_in']), p['conv'], batch).reshape(batch, seq, 3 * d)
    hf, hb = _hyena_filters(seq, p)
    fwd, inv = _dft_matrices(seq)
    hr, hi, tt = _hyena_spectrum(hf, hb, fwd)
    z, z_col0 = xs, 2 * d
    for n in range(HY_ORDER):
        y = _hyena_fwd(z, z_col0, fwd, hr, hi, tt, n)
        z, z_col0 = _hyena_inv(y, inv, z, z_col0, xs, n * d, p['skip'][n]), 0
    return z.reshape(m, d)


DENSE_FFN_PARTS = 2


def kernel(x, c, ctx, c_ctx, ada_w, ada_b, norm_g, na_w_qkv, na_q_norm, na_k_norm, na_rpb, na_w_o, gdn_w_qkv, gdn_conv, gdn_w_gate, gdn_w_beta, gdn_w_decay, gdn_a_log, gdn_dt_bias, gdn_out_norm, gdn_w_o, diff_w_qkv, diff_q_norm, diff_k_norm, diff_lambda, diff_out_norm, diff_w_o, hy_w_in, hy_conv, hy_filt_w1, hy_filt_b1, hy_filt_freq, hy_filt_w2, hy_filt_b2, hy_filt_w3, hy_filt_b3, hy_skip, hy_w_o, ffn_w_in, ffn_w_out, moe_router, moe_w_in, moe_w_out):
    batch, seq, d = x.shape
    n_ctx = ctx.shape[1]
    depth = ada_w.shape[0]
    bf = lambda w: w.astype(BF16)

    n_rows = -(-(batch + 1) // 8) * 8
    cc = jnp.zeros((n_rows, d), F32).at[:batch].set(c).at[batch].set(c_ctx)
    mods = _ada_mods(cc, ada_w, ada_b)

    xl = x.reshape(batch * seq, d)
    xc = ctx.reshape(batch * n_ctx, d)
    for i in range(depth):
        last = i == depth - 1
        kind = i % 4
        ml = [mods[i, :batch, k * d:(k + 1) * d].reshape(batch, 1, d) for k in range(6)]
        mc = [mods[i, batch:batch + 1, k * d:(k + 1) * d].reshape(1, 1, d) for k in range(6)]
        ctx_needed = (not last) or kind != 3
        hl = _norm_modulate(xl, norm_g[i, 0], ml[0], ml[1])
        hc = _norm_modulate(xc, norm_g[i, 0], mc[0], mc[1]) if ctx_needed else None
        oc = None
        if kind == 0:
            w = bf(na_w_qkv)
            qkv, qkv_c = _matmul(hl, w), _matmul(hc, w)
            gq = jnp.tile(na_q_norm, NA_HEADS) * NA_HEAD_DIM ** -0.5
            gk = jnp.tile(na_k_norm, NA_HEADS)
            ol, oc = _neighbourhood_attention(
                _head_norm(qkv, 0, gq, NA_HEAD_DIM), _head_norm(qkv, 1, gk, NA_HEAD_DIM), qkv,
                _head_norm(qkv_c, 0, gq, NA_HEAD_DIM), _head_norm(qkv_c, 1, gk, NA_HEAD_DIM), qkv_c,
                na_rpb, batch)
            w_o = bf(na_w_o)
        elif kind == 1:
            p = dict(w_qkv=bf(gdn_w_qkv), conv=gdn_conv, w_gate=bf(gdn_w_gate), w_beta=gdn_w_beta,
                     w_decay=gdn_w_decay, a_log=gdn_a_log, dt_bias=gdn_dt_bias, out_norm=gdn_out_norm)
            zeros = jnp.zeros((batch, 2, GDN_HEADS, GDN_HEAD_DIM, GDN_HEAD_DIM), F32)
            oc, s_ctx = _gdn_branch(hc, batch, zeros, p)
            ol, _ = _gdn_branch(hl, batch, s_ctx, p)
            w_o = bf(gdn_w_o)
        elif kind == 2:
            lambda_init = 0.8 - 0.6 * math.exp(-0.3 * i)
            w = bf(diff_w_qkv)
            qkv, qkv_c = _matmul(hl, w), _matmul(hc, w)
            reps = d // DIFF_HEAD_DIM
            gq = jnp.tile(diff_q_norm, reps) * DIFF_HEAD_DIM ** -0.5
            gk = jnp.tile(diff_k_norm, reps)
            ol, oc = _diff_attention(
                _head_norm(qkv, 0, gq, DIFF_HEAD_DIM, seq=seq, rope=True),
                _head_norm(qkv, 1, gk, DIFF_HEAD_DIM, seq=seq, rope=True), qkv,
                _head_norm(qkv_c, 0, gq, DIFF_HEAD_DIM), _head_norm(qkv_c, 1, gk, DIFF_HEAD_DIM), qkv_c,
                diff_lambda, diff_out_norm, lambda_init, batch)
            w_o = bf(diff_w_o)
        else:
            p = dict(w_in=bf(hy_w_in), conv=hy_conv, w1=hy_filt_w1, b1=hy_filt_b1, freq=hy_filt_freq,
                     w2=hy_filt_w2, b2=hy_filt_b2, w3=hy_filt_w3, b3=hy_filt_b3, skip=hy_skip)
            ol = _hyena_branch(hl, batch, p)
            oc = _hyena_branch(hc, batch, p) if ctx_needed else None
            w_o = bf(hy_w_o)
        xl = _matmul_residual(ol, w_o, xl, ml[2])
        if not last:
            xc = _matmul_residual(oc, w_o, xc, mc[2])

        j = i // 2
        streams = [(xl, ml)] if last else [(xl, ml), (xc, mc)]
        outs = []
        for xs, ms in streams:
            if i % 2 == 0:
                hs = _norm_modulate(xs, norm_g[i, 1], ms[3], ms[4])
                w_in = bf(ffn_w_in[j]).reshape(1, d, -1)
                w_out = bf(ffn_w_out[j]).reshape(DENSE_FFN_PARTS, -1, d)
                outs.append(_ffn(hs, w_in, w_out, xs, ms[5], dense_parts=DENSE_FFN_PARTS))
            else:
                hs, gates = _norm_modulate(xs, norm_g[i, 1], ms[3], ms[4], router=moe_router[j])
                outs.append(_ffn(hs, bf(moe_w_in[j]), bf(moe_w_out[j]), xs, ms[5], gates=gates))
        xl = outs[0]
        if not last:
            xc = outs[1]
    return xl.reshape(batch, seq, d)
```

```python
import functools
import math

import numpy as np
import jax
import jax.numpy as jnp
from jax import lax
from jax.experimental import pallas as pl
from jax.experimental.pallas import tpu as pltpu

F32 = jnp.float32
BF16 = jnp.bfloat16
HIGHEST = lax.Precision.HIGHEST

VMEM_LIMIT_BYTES = 56 * 1024 * 1024
LANES = 128

D_MODEL = 1024
GRID_W = 64
RMS_EPS = 1e-6
L2_EPS = 1e-6
ROPE_BASE = 10000.0
NA_HEADS = 16
NA_HEAD_DIM = 64
NA_WIN_ROWS = 8
NA_WIN_COLS = 16
GDN_HEADS = 8
GDN_HEAD_DIM = 128
GDN_CHUNK = 64
GDN_SOLVE_BASE = 8
DIFF_HEADS = 8
DIFF_HEAD_DIM = 64
HY_ORDER = 2
HY_EMB_DIM = 33
HY_MAX_DECAY = math.log(1e-2) / 0.3
HY_MIN_DECAY = math.log(1e-2) / 1.5
MOE_EXPERTS = 8
NEG_BIG = -1e30


def _params(*sem):
    return pltpu.CompilerParams(dimension_semantics=sem, vmem_limit_bytes=VMEM_LIMIT_BYTES)


def _dot(a, b):
    return jnp.dot(a, b, preferred_element_type=F32)


def _dot_t(a, b):
    return lax.dot_general(a, b, (((1,), (1,)), ((), ())), preferred_element_type=F32)


def _silu(x):
    return x * (1.0 / (1.0 + jnp.exp(-x)))


def _ada_kernel(c_ref, w_ref, b_ref, o_ref):
    s = _silu(c_ref[...])
    o_ref[0] = jnp.dot(s, w_ref[0], preferred_element_type=F32, precision=HIGHEST) + b_ref[0]


def _ada_mods(cc, ada_w, ada_b):
    depth, d, n = ada_w.shape
    r = cc.shape[0]
    tn = 1024
    return pl.pallas_call(
        _ada_kernel,
        out_shape=jax.ShapeDtypeStruct((depth, r, n), F32),
        grid=(depth, n // tn),
        in_specs=[pl.BlockSpec((r, d), lambda i, j: (0, 0)),
                  pl.BlockSpec((1, d, tn), lambda i, j: (i, 0, j)),
                  pl.BlockSpec((1, 1, tn), lambda i, j: (i, 0, j))],
        out_specs=pl.BlockSpec((1, r, tn), lambda i, j: (i, 0, j)),
        compiler_params=_params("arbitrary", "arbitrary"),
        name="ada_mods",
    )(cc, ada_w, ada_b.reshape(depth, 1, n))


def _norm_mod(x_ref, g_ref, sh_ref, sc_ref):
    x = x_ref[...]
    y = x * lax.rsqrt(jnp.mean(x * x, axis=-1, keepdims=True) + RMS_EPS) * g_ref[...]
    return y * (1.0 + sc_ref[0]) + sh_ref[0]


def _norm_mod_kernel(x_ref, g_ref, sh_ref, sc_ref, o_ref):
    o_ref[...] = _norm_mod(x_ref, g_ref, sh_ref, sc_ref).astype(o_ref.dtype)


def _norm_mod_route_kernel(x_ref, g_ref, sh_ref, sc_ref, r_ref, o_ref, gate_ref):
    h = _norm_mod(x_ref, g_ref, sh_ref, sc_ref)
    o_ref[...] = h.astype(o_ref.dtype)
    logits = jnp.dot(h, r_ref[...], preferred_element_type=F32, precision=HIGHEST)
    lane = lax.broadcasted_iota(jnp.int32, logits.shape, 1).astype(F32)
    l1 = jnp.where(lane < MOE_EXPERTS, logits, NEG_BIG)
    m1 = jnp.max(l1, axis=-1, keepdims=True)
    i1 = jnp.min(jnp.where(l1 == m1, lane, float(LANES)), axis=-1, keepdims=True)
    l2 = jnp.where(lane == i1, NEG_BIG, l1)
    m2 = jnp.max(l2, axis=-1, keepdims=True)
    i2 = jnp.min(jnp.where(l2 == m2, lane, float(LANES)), axis=-1, keepdims=True)
    e2 = jnp.exp(m2 - m1)
    w1 = 1.0 / (1.0 + e2)
    gate_ref[...] = jnp.where(lane == i1, w1, 0.0) + jnp.where(lane == i2, e2 * w1, 0.0)


def _norm_modulate(x, g, shift, scale, router=None, tm=512):
    m, d = x.shape
    grp = shift.shape[0]
    rows = m // grp
    tm = min(tm, rows)
    assert rows % tm == 0
    in_specs = [pl.BlockSpec((tm, d), lambda i: (i, 0)),
                pl.BlockSpec((1, d), lambda i: (0, 0)),
                pl.BlockSpec((1, 1, d), lambda i: (i * tm // rows, 0, 0)),
                pl.BlockSpec((1, 1, d), lambda i: (i * tm // rows, 0, 0))]
    args = [x, g.reshape(1, d), shift, scale]
    if router is None:
        return pl.pallas_call(
            _norm_mod_kernel, out_shape=jax.ShapeDtypeStruct((m, d), BF16), grid=(m // tm,),
            in_specs=in_specs, out_specs=pl.BlockSpec((tm, d), lambda i: (i, 0)),
            compiler_params=_params("parallel"), name="norm_mod")(*args)
    rpad = jnp.zeros((d, LANES), F32).at[:, :router.shape[1]].set(router)
    return pl.pallas_call(
        _norm_mod_route_kernel,
        out_shape=(jax.ShapeDtypeStruct((m, d), BF16), jax.ShapeDtypeStruct((m, LANES), F32)),
        grid=(m // tm,),
        in_specs=in_specs + [pl.BlockSpec((d, LANES), lambda i: (0, 0))],
        out_specs=(pl.BlockSpec((tm, d), lambda i: (i, 0)), pl.BlockSpec((tm, LANES), lambda i: (i, 0))),
        compiler_params=_params("parallel"), name="norm_mod_route")(*args, rpad)


def _mm_kernel(x_ref, w_ref, o_ref):
    o_ref[...] = _dot(x_ref[...], w_ref[...]).astype(o_ref.dtype)


def _mm_res_kernel(x_ref, w_ref, res_ref, gate_ref, o_ref):
    o_ref[...] = res_ref[...] + gate_ref[0] * _dot(x_ref[...], w_ref[...])


def _matmul(x, w, out_dtype=BF16, tm=1024, tn=1024):
    m, k = x.shape
    n = w.shape[1]
    tm, tn = min(tm, m), min(tn, n)
    assert m % tm == 0 and n % tn == 0
    return pl.pallas_call(
        _mm_kernel, out_shape=jax.ShapeDtypeStruct((m, n), out_dtype), grid=(m // tm, n // tn),
        in_specs=[pl.BlockSpec((tm, k), lambda i, j: (i, 0)), pl.BlockSpec((k, tn), lambda i, j: (0, j))],
        out_specs=pl.BlockSpec((tm, tn), lambda i, j: (i, j)),
        compiler_params=_params("parallel", "arbitrary"), name="matmul")(x, w)


def _matmul_residual(x, w, res, gate, tm=1024, tn=1024):
    m, k = x.shape
    n = w.shape[1]
    rows = m // gate.shape[0]
    tm, tn = min(tm, rows), min(tn, n)
    assert rows % tm == 0 and n % tn == 0
    return pl.pallas_call(
        _mm_res_kernel, out_shape=jax.ShapeDtypeStruct((m, n), F32), grid=(m // tm, n // tn),
        in_specs=[pl.BlockSpec((tm, k), lambda i, j: (i, 0)), pl.BlockSpec((k, tn), lambda i, j: (0, j)),
                  pl.BlockSpec((tm, tn), lambda i, j: (i, j)),
                  pl.BlockSpec((1, 1, tn), lambda i, j: (i * tm // rows, 0, j))],
        out_specs=pl.BlockSpec((tm, tn), lambda i, j: (i, j)),
        compiler_params=_params("parallel", "arbitrary"), name="matmul_residual")(x, w, res, gate)


def _ffn_kernel(*refs, gated):
    if gated:
        h_ref, wa_ref, wb_ref, wo_ref, res_ref, mod_ref, gate_ref, o_ref, acc_ref = refs
    else:
        h_ref, wa_ref, wb_ref, wo_ref, res_ref, mod_ref, o_ref, acc_ref = refs
    e = pl.program_id(1)
    h = h_ref[...]
    a = _dot(h, wa_ref[0])
    b = _dot(h, wb_ref[0])
    y = _dot((_silu(a) * b).astype(BF16), wo_ref[0])
    if gated:
        g = gate_ref[...]
        lane = lax.broadcasted_iota(jnp.int32, g.shape, 1)
        y = y * jnp.sum(jnp.where(lane == e, g, 0.0), axis=-1, keepdims=True)

    @pl.when(e == 0)
    def _():
        acc_ref[...] = y

    @pl.when(e > 0)
    def _():
        acc_ref[...] += y

    @pl.when(e == pl.num_programs(1) - 1)
    def _():
        o_ref[...] = res_ref[...] + mod_ref[0] * acc_ref[...]


def _ffn(h, w_in, w_out, res, mod, gates=None, dense_parts=0, tm=512):
    m, d = h.shape
    rows = m // mod.shape[0]
    tm = min(tm, rows)
    assert rows % tm == 0
    if gates is None:
        n_e, f = dense_parts, w_out.shape[1]
        a_map, b_map = (lambda i, e: (0, 0, e)), (lambda i, e: (0, 0, n_e + e))
    else:
        n_e, f = w_in.shape[0], w_out.shape[1]
        a_map, b_map = (lambda i, e: (e, 0, 0)), (lambda i, e: (e, 0, 1))
    in_specs = [pl.BlockSpec((tm, d), lambda i, e: (i, 0)),
                pl.BlockSpec((1, d, f), a_map),
                pl.BlockSpec((1, d, f), b_map),
                pl.BlockSpec((1, f, d), lambda i, e: (e, 0, 0)),
                pl.BlockSpec((tm, d), lambda i, e: (i, 0)),
                pl.BlockSpec((1, 1, d), lambda i, e: (i * tm // rows, 0, 0))]
    args = [h, w_in, w_in, w_out, res, mod]
    if gates is not None:
        in_specs.append(pl.BlockSpec((tm, LANES), lambda i, e: (i, 0)))
        args.append(gates)
    return pl.pallas_call(
        functools.partial(_ffn_kernel, gated=gates is not None),
        out_shape=jax.ShapeDtypeStruct((m, d), F32), grid=(m // tm, n_e),
        in_specs=in_specs, out_specs=pl.BlockSpec((tm, d), lambda i, e: (i, 0)),
        scratch_shapes=[pltpu.VMEM((tm, d), F32)],
        compiler_params=_params("parallel", "arbitrary"), name="ffn")(*args)


def _block_diag_ones(d, group):
    idx = np.arange(d) // group
    return jnp.asarray((idx[:, None] == idx[None, :]).astype(np.float32)).astype(BF16)


def _head_norm_kernel(*refs, group, rope):
    if rope:
        x_ref, gain_ref, bd_ref, cos_ref, sa_ref, sb_ref, o_ref = refs
    else:
        x_ref, gain_ref, bd_ref, o_ref = refs
    x = x_ref[...].astype(F32)
    ss = _dot((x * x).astype(BF16), bd_ref[...])
    y = x * lax.rsqrt(ss * (1.0 / group) + RMS_EPS) * gain_ref[...]
    if rope:
        d = y.shape[-1]
        rep = d // LANES
        half = group // 4
        y = (y * pltpu.repeat(cos_ref[...], rep, axis=1)
             + pltpu.roll(y, d - half, 1) * pltpu.repeat(sa_ref[...], rep, axis=1)
             + pltpu.roll(y, half, 1) * pltpu.repeat(sb_ref[...], rep, axis=1))
    o_ref[...] = y.astype(o_ref.dtype)


def _rope_tables(seq):
    pos = np.arange(seq)
    lane = np.arange(LANES) % DIFF_HEAD_DIM
    half = DIFF_HEAD_DIM // 2
    j = lane % half
    inv = ROPE_BASE ** (-(2.0 * (j % (half // 2))) / half)
    p = np.where(lane[None, :] < half, (pos // GRID_W)[:, None], (pos % GRID_W)[:, None])
    ang = p * inv[None, :]
    first = (j < half // 2)[None, :]
    cos, sin = np.cos(ang), np.sin(ang)
    return (jnp.asarray(cos, F32), jnp.asarray(np.where(first, -sin, 0.0), F32),
            jnp.asarray(np.where(first, 0.0, sin), F32))


def _head_norm(src, col_block, gain, group, seq=None, rope=False, tm=512):
    m = src.shape[0]
    d = D_MODEL
    tm = min(tm, m if seq is None else seq)
    assert m % tm == 0
    in_specs = [pl.BlockSpec((tm, d), lambda i: (i, col_block)),
                pl.BlockSpec((1, d), lambda i: (0, 0)),
                pl.BlockSpec((d, d), lambda i: (0, 0))]
    args = [src, gain.reshape(1, d), _block_diag_ones(d, group)]
    if rope:
        nblk = seq // tm
        in_specs += [pl.BlockSpec((tm, LANES), lambda i: (i % nblk, 0))] * 3
        args += list(_rope_tables(seq))
    return pl.pallas_call(
        functools.partial(_head_norm_kernel, group=group, rope=rope),
        out_shape=jax.ShapeDtypeStruct((m, d), BF16), grid=(m // tm,),
        in_specs=in_specs, out_specs=pl.BlockSpec((tm, d), lambda i: (i, 0)),
        compiler_params=_params("parallel"), name="head_norm")(*args)


def _na_bias_table(rpb, rows):
    wr = min(NA_WIN_ROWS, rows)
    cols = np.arange(GRID_W)
    c0 = np.clip(cols - NA_WIN_COLS // 2, 0, GRID_W - NA_WIN_COLS)
    col_valid = (cols[None, :] >= c0[:, None]) & (cols[None, :] < c0[:, None] + NA_WIN_COLS)
    col_idx = np.clip(cols[None, :] - cols[:, None], 1 - NA_WIN_COLS, NA_WIN_COLS - 1) + NA_WIN_COLS - 1
    variants = []
    for off in range(wr):
        row_idx = np.arange(wr) - off + NA_WIN_ROWS - 1
        bias = jnp.transpose(rpb[:, row_idx][:, :, col_idx], (0, 2, 1, 3))
        bias = jnp.where(col_valid[None, :, None, :], bias, NEG_BIG)
        variants.append(bias.reshape(NA_HEADS, GRID_W, wr * GRID_W))
    return jnp.stack(variants).astype(F32)


def _na_row_start(r, rows, wr):
    return jnp.clip(r - wr // 2, 0, rows - wr)


def _na_kernel(q_ref, k_ref, v_ref, kc_ref, vc_ref, bias_ref, o_ref, *, rows, wr):
    r = pl.program_id(1)
    start = pl.multiple_of(_na_row_start(r, rows, wr) * GRID_W, GRID_W)
    lane = lax.broadcasted_iota(jnp.int32, (GRID_W, LANES), 1)
    low = lane < NA_HEAD_DIM
    win = pl.ds(start, wr * GRID_W)
    cols = lambda h: slice((h // 2) * LANES, (h // 2 + 1) * LANES)
    heads = range(NA_HEADS)
    scores = []
    for h in heads:
        q = q_ref[0, :, cols(h)]
        qm = jnp.where(low if h % 2 == 0 else ~low, q, jnp.zeros_like(q))
        scores.append((_dot_t(qm, k_ref[0, win, cols(h)]) + bias_ref[0, h], _dot_t(qm, kc_ref[0, :, cols(h)])))
    probs = []
    for s_loc, s_ctx in scores:
        m = jnp.maximum(jnp.max(s_loc, axis=-1, keepdims=True), jnp.max(s_ctx, axis=-1, keepdims=True))
        p_loc = jnp.exp(s_loc - m)
        p_ctx = jnp.exp(s_ctx - m)
        z = jnp.sum(p_loc, axis=-1, keepdims=True) + jnp.sum(p_ctx, axis=-1, keepdims=True)
        probs.append((p_loc.astype(BF16), p_ctx.astype(BF16), 1.0 / z))
    outs = [(_dot(p_loc, v_ref[0, win, cols(h)]) + _dot(p_ctx, vc_ref[0, :, cols(h)])) * rz
            for h, (p_loc, p_ctx, rz) in zip(heads, probs)]
    for pair in range(NA_HEADS // 2):
        o_ref[0, :, cols(2 * pair)] = jnp.where(low, outs[2 * pair], outs[2 * pair + 1]).astype(o_ref.dtype)


def _na_ctx_kernel(q_ref, k_ref, v_ref, o_ref):
    lane = lax.broadcasted_iota(jnp.int32, (q_ref.shape[1], LANES), 1)
    low = lane < NA_HEAD_DIM
    for pair in range(NA_HEADS // 2):
        cs = slice(pair * LANES, (pair + 1) * LANES)
        q, k, v = q_ref[0, :, cs], k_ref[0, :, cs], v_ref[0, :, cs]
        outs = []
        for sub in range(2):
            qm = jnp.where(low if sub == 0 else ~low, q, jnp.zeros_like(q))
            s = _dot_t(qm, k)
            p = jnp.exp(s - jnp.max(s, axis=-1, keepdims=True))
            outs.append(_dot(p.astype(BF16), v) * (1.0 / jnp.sum(p, axis=-1, keepdims=True)))
        o_ref[0, :, cs] = jnp.where(low, outs[0], outs[1]).astype(o_ref.dtype)


def _neighbourhood_attention(q, k, qkv, qc, kc, qkv_c, rpb, batch):
    d = D_MODEL
    s = q.shape[0] // batch
    n_ctx = qc.shape[0] // batch
    rows = s // GRID_W
    wr = min(NA_WIN_ROWS, rows)
    bias = _na_bias_table(rpb, rows)
    q3, k3, qkv3 = q.reshape(batch, s, d), k.reshape(batch, s, d), qkv.reshape(batch, s, 3 * d)
    qc3, kc3, qkvc3 = qc.reshape(batch, n_ctx, d), kc.reshape(batch, n_ctx, d), qkv_c.reshape(batch, n_ctx, 3 * d)

    def variant(b, r):
        return (r - _na_row_start(r, rows, wr), 0, 0, 0)

    ol = pl.pallas_call(
        functools.partial(_na_kernel, rows=rows, wr=wr),
        out_shape=jax.ShapeDtypeStruct((batch, s, d), BF16), grid=(batch, rows),
        in_specs=[pl.BlockSpec((1, GRID_W, d), lambda b, r: (b, r, 0)),
                  pl.BlockSpec((1, s, d), lambda b, r: (b, 0, 0)),
                  pl.BlockSpec((1, s, d), lambda b, r: (b, 0, 2)),
                  pl.BlockSpec((1, n_ctx, d), lambda b, r: (b, 0, 0)),
                  pl.BlockSpec((1, n_ctx, d), lambda b, r: (b, 0, 2)),
                  pl.BlockSpec((1, NA_HEADS, GRID_W, wr * GRID_W), variant)],
        out_specs=pl.BlockSpec((1, GRID_W, d), lambda b, r: (b, r, 0)),
        compiler_params=_params("parallel", "arbitrary"), name="na_attention",
    )(q3, k3, qkv3, kc3, qkvc3, bias)
    oc = pl.pallas_call(
        _na_ctx_kernel, out_shape=jax.ShapeDtypeStruct((batch, n_ctx, d), BF16), grid=(batch,),
        in_specs=[pl.BlockSpec((1, n_ctx, d), lambda b: (b, 0, 0)),
                  pl.BlockSpec((1, n_ctx, d), lambda b: (b, 0, 0)),
                  pl.BlockSpec((1, n_ctx, d), lambda b: (b, 0, 2))],
        out_specs=pl.BlockSpec((1, n_ctx, d), lambda b: (b, 0, 0)),
        compiler_params=_params("parallel"), name="na_ctx_attention",
    )(qc3, kc3, qkvc3)
    return ol.reshape(batch * s, d), oc.reshape(batch * n_ctx, d)


def _diff_lambda(lam_ref, lambda_init):
    lv = lam_ref[...]
    a = jnp.sum(lv[0:1] * lv[1:2], axis=-1, keepdims=True)
    b = jnp.sum(lv[2:3] * lv[3:4], axis=-1, keepdims=True)
    return jnp.exp(a) - jnp.exp(b) + lambda_init


def _diff_out(o, onorm_ref, lambda_init, o_ref):
    y = o * lax.rsqrt(jnp.mean(o * o, axis=-1, keepdims=True) + RMS_EPS) * onorm_ref[...]
    o_ref[0] = (y * (1.0 - lambda_init)).astype(o_ref.dtype)


def _diff_kernel(q_ref, k_ref, v_ref, kc_ref, vc_ref, lam_ref, onorm_ref, o_ref, *, lambda_init):
    lam = _diff_lambda(lam_ref, lambda_init)
    q = q_ref[0]
    k, v, kc, vc = k_ref[0], v_ref[0], kc_ref[0], vc_ref[0]
    low = lax.broadcasted_iota(jnp.int32, q.shape, 1) < DIFF_HEAD_DIM
    a_loc = a_ctx = None
    for sub in range(2):
        qm = jnp.where(low if sub == 0 else ~low, q, jnp.zeros_like(q))
        s_loc = _dot_t(qm, k)
        s_ctx = _dot_t(qm, kc)
        m = jnp.maximum(jnp.max(s_loc, axis=-1, keepdims=True), jnp.max(s_ctx, axis=-1, keepdims=True))
        p_loc = jnp.exp(s_loc - m)
        p_ctx = jnp.exp(s_ctx - m)
        z = jnp.sum(p_loc, axis=-1, keepdims=True) + jnp.sum(p_ctx, axis=-1, keepdims=True)
        w = (1.0 / z) if sub == 0 else (-lam / z)
        a_loc = p_loc * w if sub == 0 else a_loc + p_loc * w
        a_ctx = p_ctx * w if sub == 0 else a_ctx + p_ctx * w
    o = _dot(a_loc.astype(BF16), v) + _dot(a_ctx.astype(BF16), vc)
    _diff_out(o, onorm_ref, lambda_init, o_ref)


def _diff_ctx_kernel(q_ref, k_ref, v_ref, lam_ref, onorm_ref, o_ref, *, lambda_init):
    lam = _diff_lambda(lam_ref, lambda_init)
    q, k, v = q_ref[0], k_ref[0], v_ref[0]
    low = lax.broadcasted_iota(jnp.int32, q.shape, 1) < DIFF_HEAD_DIM
    a = None
    for sub in range(2):
        qm = jnp.where(low if sub == 0 else ~low, q, jnp.zeros_like(q))
        s = _dot_t(qm, k)
        p = jnp.exp(s - jnp.max(s, axis=-1, keepdims=True))
        z = jnp.sum(p, axis=-1, keepdims=True)
        a = p * (1.0 / z) if sub == 0 else a - p * (lam / z)
    _diff_out(_dot(a.astype(BF16), v), onorm_ref, lambda_init, o_ref)


def _diff_attention(q, k, qkv, qc, kc, qkv_c, lam_vecs, out_norm, lambda_init, batch, tq=256):
    d = D_MODEL
    hw = 2 * DIFF_HEAD_DIM
    s = q.shape[0] // batch
    n_ctx = qc.shape[0] // batch
    v_blk = 2 * d // hw
    q3, k3, qkv3 = q.reshape(batch, s, d), k.reshape(batch, s, d), qkv.reshape(batch, s, 3 * d)
    qc3, kc3, qkvc3 = qc.reshape(batch, n_ctx, d), kc.reshape(batch, n_ctx, d), qkv_c.reshape(batch, n_ctx, 3 * d)
    onorm = out_norm.reshape(1, hw)
    ol = pl.pallas_call(
        functools.partial(_diff_kernel, lambda_init=lambda_init),
        out_shape=jax.ShapeDtypeStruct((batch, s, d), BF16), grid=(batch, DIFF_HEADS, s // tq),
        in_specs=[pl.BlockSpec((1, tq, hw), lambda b, h, i: (b, i, h)),
                  pl.BlockSpec((1, s, hw), lambda b, h, i: (b, 0, h)),
                  pl.BlockSpec((1, s, hw), lambda b, h, i: (b, 0, v_blk + h)),
                  pl.BlockSpec((1, n_ctx, hw), lambda b, h, i: (b, 0, h)),
                  pl.BlockSpec((1, n_ctx, hw), lambda b, h, i: (b, 0, v_blk + h)),
                  pl.BlockSpec((4, DIFF_HEAD_DIM), lambda b, h, i: (0, 0)),
                  pl.BlockSpec((1, hw), lambda b, h, i: (0, 0))],
        out_specs=pl.BlockSpec((1, tq, hw), lambda b, h, i: (b, i, h)),
        compiler_params=_params("parallel", "arbitrary", "arbitrary"), name="diff_attention",
    )(q3, k3, qkv3, kc3, qkvc3, lam_vecs, onorm)
    oc = pl.pallas_call(
        functools.partial(_diff_ctx_kernel, lambda_init=lambda_init),
        out_shape=jax.ShapeDtypeStruct((batch, n_ctx, d), BF16), grid=(batch, DIFF_HEADS),
        in_specs=[pl.BlockSpec((1, n_ctx, hw), lambda b, h: (b, 0, h)),
                  pl.BlockSpec((1, n_ctx, hw), lambda b, h: (b, 0, h)),
                  pl.BlockSpec((1, n_ctx, hw), lambda b, h: (b, 0, v_blk + h)),
                  pl.BlockSpec((4, DIFF_HEAD_DIM), lambda b, h: (0, 0)),
                  pl.BlockSpec((1, hw), lambda b, h: (0, 0))],
        out_specs=pl.BlockSpec((1, n_ctx, hw), lambda b, h: (b, 0, h)),
        compiler_params=_params("parallel", "arbitrary"), name="diff_ctx_attention",
    )(qc3, kc3, qkvc3, lam_vecs, onorm)
    return ol.reshape(batch * s, d), oc.reshape(batch * n_ctx, d)


def _conv_kernel(x_ref, w_ref, o_ref, *, act, n_norm_q, n_norm, qscale):
    x = x_ref[0].astype(F32)
    seq = x.shape[0]
    row = lax.broadcasted_iota(jnp.int32, x.shape, 0)
    prev = jnp.where(row == 0, 0.0, pltpu.roll(x, 1, 0))
    nxt = jnp.where(row == seq - 1, 0.0, pltpu.roll(x, seq - 1, 0))
    w = w_ref[...]
    y = prev * w[0:1] + x * w[1:2] + nxt * w[2:3]
    if act:
        y = _silu(y)
    if n_norm:
        j = pl.program_id(1)
        parts = []
        for g in range(y.shape[1] // LANES):
            seg = y[:, g * LANES:(g + 1) * LANES]
            inv = lax.rsqrt(jnp.sum(seg * seg, axis=-1, keepdims=True) + L2_EPS)
            scale = jnp.where(j < n_norm_q, inv * qscale, jnp.where(j < n_norm, inv, 1.0))
            parts.append(seg * scale)
        y = jnp.concatenate(parts, axis=-1) if len(parts) > 1 else parts[0]
    o_ref[0] = y.astype(o_ref.dtype)


def _dwconv3(x, w, batch, act=False, l2norm_cols=0, qscale=1.0, tn=256):
    m, c = x.shape
    seq = m // batch
    n_norm = l2norm_cols // tn
    out = pl.pallas_call(
        functools.partial(_conv_kernel, act=act, n_norm_q=n_norm // 2, n_norm=n_norm, qscale=qscale),
        out_shape=jax.ShapeDtypeStruct((batch, seq, c), BF16), grid=(batch, c // tn),
        in_specs=[pl.BlockSpec((1, seq, tn), lambda b, j: (b, 0, j)),
                  pl.BlockSpec((3, tn), lambda b, j: (0, j))],
        out_specs=pl.BlockSpec((1, seq, tn), lambda b, j: (b, 0, j)),
        compiler_params=_params("parallel", "arbitrary"), name="dwconv3",
    )(x.reshape(batch, seq, c), w)
    return out.reshape(m, c)


def _gdn_gate_kernel(h_ref, w_ref, wt_ref, a_ref, at_ref, bias_ref, biast_ref, col_ref, row_ref):
    h = h_ref[...]
    col = _dot(h, w_ref[...])
    row = _dot_t(wt_ref[...], h)

    def act(z, neg_a, bias, is_g):
        zb = z + bias
        softplus = jnp.maximum(zb, 0.0) + jnp.log(1.0 + jnp.exp(-jnp.abs(zb)))
        return jnp.where(is_g, neg_a * softplus, 1.0 / (1.0 + jnp.exp(-z)))

    lane = lax.broadcasted_iota(jnp.int32, col.shape, 1)
    col_ref[...] = act(col, a_ref[...], bias_ref[...], lane < 2 * GDN_HEADS)
    sub = lax.broadcasted_iota(jnp.int32, row.shape, 0)
    row_ref[...] = act(row, at_ref[...], biast_ref[...], sub < 2 * GDN_HEADS)


def _gdn_gates(h, w_beta, w_decay, a_log, dt_bias, tm=512):
    m, d = h.shape
    nh = GDN_HEADS
    tm = min(tm, m)
    w = jnp.concatenate([w_decay[0], w_decay[1], w_beta[0], w_beta[1]], axis=-1)
    wpad = jnp.zeros((d, LANES), F32).at[:, :4 * nh].set(w).astype(BF16)
    wt = w.T.astype(BF16)
    neg_a = jnp.concatenate([-jnp.exp(a_log[0]), -jnp.exp(a_log[1]), jnp.zeros((2 * nh,), F32)])
    bias = jnp.concatenate([dt_bias[0], dt_bias[1], jnp.zeros((2 * nh,), F32)])
    pad = lambda v: jnp.zeros((1, LANES), F32).at[0, :4 * nh].set(v)
    return pl.pallas_call(
        _gdn_gate_kernel,
        out_shape=(jax.ShapeDtypeStruct((m, LANES), F32), jax.ShapeDtypeStruct((4 * nh, m), F32)),
        grid=(m // tm,),
        in_specs=[pl.BlockSpec((tm, d), lambda i: (i, 0)),
                  pl.BlockSpec((d, LANES), lambda i: (0, 0)),
                  pl.BlockSpec((4 * nh, d), lambda i: (0, 0)),
                  pl.BlockSpec((1, LANES), lambda i: (0, 0)),
                  pl.BlockSpec((4 * nh, 1), lambda i: (0, 0)),
                  pl.BlockSpec((1, LANES), lambda i: (0, 0)),
                  pl.BlockSpec((4 * nh, 1), lambda i: (0, 0))],
        out_specs=(pl.BlockSpec((tm, LANES), lambda i: (i, 0)), pl.BlockSpec((4 * nh, tm), lambda i: (0, i))),
        compiler_params=_params("parallel"), name="gdn_gates",
    )(h, wpad, wt, pad(neg_a), neg_a.reshape(4 * nh, 1), pad(bias), bias.reshape(4 * nh, 1))


def _gdn_local_kernel(q_ref, k_ref, v_ref, col_ref, row_ref,
                      u_ref, w_ref, qg_ref, kd_ref, att_ref, dl_ref):
    c = GDN_CHUNK
    nh = GDN_HEADS
    hd = GDN_HEAD_DIM
    ii = lax.broadcasted_iota(jnp.int32, (c, c), 0)
    jj = lax.broadcasted_iota(jnp.int32, (c, c), 1)
    eye = (ii == jj).astype(F32)
    incl = [ii >= jj, ii <= jj]
    strict = [ii > jj, ii < jj]
    col = col_ref[0]
    row = row_ref[0, 0]
    gc_col, gc_row = [], []
    for d in range(2):
        m_col = incl[d].astype(F32)
        m_row = incl[1 - d].astype(F32)
        gc_col.append(jnp.dot(m_col, col[:, d * nh:(d + 1) * nh], preferred_element_type=F32, precision=HIGHEST))
        gc_row.append(jnp.dot(row[d * nh:(d + 1) * nh], m_row, preferred_element_type=F32, precision=HIGHEST))
    base = GDN_SOLVE_BASE
    same_base = (ii // base) == (jj // base)
    a_mats, rhs, tails = [], [], []
    for h in range(nh):
        cs = slice(h * hd, (h + 1) * hd)
        q, k, v = q_ref[0, :, cs], k_ref[0, :, cs], v_ref[0, :, cs]
        qf, kf, vf = q.astype(F32), k.astype(F32), v.astype(F32)
        kk = _dot_t(k, k)
        qk = _dot_t(q, k)
        for d in range(2):
            gcc = gc_col[d][:, h:h + 1]
            gcr = gc_row[d][h:h + 1, :]
            beta = col[:, 2 * nh + d * nh + h:2 * nh + d * nh + h + 1]
            dec = jnp.exp(jnp.where(incl[d], gcc - gcr, NEG_BIG))
            a_mats.append(jnp.where(strict[d], beta * kk * dec, 0.0))
            eg = jnp.exp(gcc)
            rhs.append(jnp.concatenate([vf * beta, kf * (beta * eg)], axis=-1).astype(BF16))
            g_last = gcr[:, c - 1:c] if d == 0 else gcr[:, 0:1]
            qg_ref[d, 0, :, cs] = (qf * eg).astype(qg_ref.dtype)
            kd_ref[d, 0, :, cs] = (kf * jnp.exp(g_last - gcc)).astype(kd_ref.dtype)
            att_ref[d, 0, 0, h] = (qk * dec).astype(att_ref.dtype)
            dl_ref[d, 0, 0, h:h + 1, :] = jnp.broadcast_to(jnp.exp(g_last), (1, hd))
            tails.append((d, cs))
    bdot = lambda x, y: _dot(x.astype(BF16), y.astype(BF16))
    npow = [jnp.where(same_base, -a, 0.0) for a in a_mats]
    inv = [eye + n for n in npow]
    span = 2
    while span < base:
        npow = [bdot(n, n) for n in npow]
        inv = [p + bdot(p, n) for p, n in zip(inv, npow)]
        span *= 2
    size = base
    while size < c:
        merge = jnp.logical_and((ii // (2 * size)) == (jj // (2 * size)), (ii // size) != (jj // size))
        low = [bdot(jnp.where(merge, a, 0.0), p) for a, p in zip(a_mats, inv)]
        inv = [p - bdot(p, x) for p, x in zip(inv, low)]
        size *= 2
    for p, r, (d, cs) in zip(inv, rhs, tails):
        sol = _dot(p.astype(BF16), r)
        u_ref[d, 0, :, cs] = sol[:, :hd]
        w_ref[d, 0, :, cs] = sol[:, hd:].astype(w_ref.dtype)


def _gdn_scan_kernel(uf_ref, wf_ref, qgf_ref, kdf_ref, attf_ref, dlf_ref,
                     ub_ref, wb_ref, qgb_ref, kdb_ref, attb_ref, dlb_ref, s0_ref,
                     of_ref, ob_ref, s_ref):
    hd = GDN_HEAD_DIM

    @pl.when(pl.program_id(1) == 0)
    def _():
        s_ref[...] = s0_ref[...]

    dirs = ((uf_ref, wf_ref, qgf_ref, kdf_ref, attf_ref, dlf_ref, of_ref),
            (ub_ref, wb_ref, qgb_ref, kdb_ref, attb_ref, dlb_ref, ob_ref))
    pairs = [(d, h) for d in range(2) for h in range(GDN_HEADS)]
    cols = lambda h: slice(h * hd, (h + 1) * hd)
    states = [s_ref[0, d, h] for d, h in pairs]
    sbs = [s.astype(BF16) for s in states]
    v_new = [dirs[d][0][0, 0, :, cols(h)] - _dot(dirs[d][1][0, 0, :, cols(h)], sb)
             for (d, h), sb in zip(pairs, sbs)]
    o_state = [_dot(dirs[d][2][0, 0, :, cols(h)], sb) for (d, h), sb in zip(pairs, sbs)]
    vbs = [v.astype(BF16) for v in v_new]
    for (d, h), o1, vb, s in zip(pairs, o_state, vbs, states):
        dirs[d][6][0, :, cols(h)] = o1 + _dot(dirs[d][4][0, 0, 0, h], vb)
        upd = lax.dot_general(dirs[d][3][0, 0, :, cols(h)], vb, (((0,), (0,)), ((), ())),
                              preferred_element_type=F32)
        s_ref[0, d, h] = s * dirs[d][5][0, 0, 0, h:h + 1, :] + upd


def _gdn_core(u3, gcol, grow, s0, batch):
    c = GDN_CHUNK
    nh, hd = GDN_HEADS, GDN_HEAD_DIM
    d = nh * hd
    m = u3.shape[0]
    seq = m // batch
    n = seq // c
    u33 = u3.reshape(batch, seq, 3 * d)
    gcol3 = gcol.reshape(batch, seq, LANES)
    grow4 = jnp.transpose(grow.reshape(4 * nh, batch, n, c), (1, 2, 0, 3))
    big = lambda dt: jax.ShapeDtypeStruct((2, batch, seq, d), dt)
    blk = pl.BlockSpec((2, 1, c, d), lambda b, i: (0, b, i, 0))
    u, w, qg, kd, att, dl = pl.pallas_call(
        _gdn_local_kernel,
        out_shape=(big(F32), big(BF16), big(BF16), big(BF16),
                   jax.ShapeDtypeStruct((2, batch, n, nh, c, c), BF16),
                   jax.ShapeDtypeStruct((2, batch, n, nh, hd), F32)),
        grid=(batch, n),
        in_specs=[pl.BlockSpec((1, c, d), lambda b, i: (b, i, 0)),
                  pl.BlockSpec((1, c, d), lambda b, i: (b, i, 1)),
                  pl.BlockSpec((1, c, d), lambda b, i: (b, i, 2)),
                  pl.BlockSpec((1, c, LANES), lambda b, i: (b, i, 0)),
                  pl.BlockSpec((1, 1, 4 * nh, c), lambda b, i: (b, i, 0, 0))],
        out_specs=(blk, blk, blk, blk,
                   pl.BlockSpec((2, 1, 1, nh, c, c), lambda b, i: (0, b, i, 0, 0, 0)),
                   pl.BlockSpec((2, 1, 1, nh, hd), lambda b, i: (0, b, i, 0, 0))),
        compiler_params=_params("parallel", "arbitrary"), name="gdn_local",
    )(u33, u33, u33, gcol3, grow4)

    def dir_specs(dd):
        pos = (lambda i: i) if dd == 0 else (lambda i: n - 1 - i)
        big_blk = pl.BlockSpec((1, 1, c, d), lambda b, i: (dd, b, pos(i), 0))
        return [big_blk, big_blk, big_blk, big_blk,
                pl.BlockSpec((1, 1, 1, nh, c, c), lambda b, i: (dd, b, pos(i), 0, 0, 0)),
                pl.BlockSpec((1, 1, 1, nh, hd), lambda b, i: (dd, b, pos(i), 0, 0))]

    s_blk = pl.BlockSpec((1, 2, nh, hd, hd), lambda b, i: (b, 0, 0, 0, 0))
    o_f, o_b, s_fin = pl.pallas_call(
        _gdn_scan_kernel,
        out_shape=(jax.ShapeDtypeStruct((batch, seq, d), F32), jax.ShapeDtypeStruct((batch, seq, d), F32),
                   jax.ShapeDtypeStruct((batch, 2, nh, hd, hd), F32)),
        grid=(batch, n),
        in_specs=dir_specs(0) + dir_specs(1) + [s_blk],
        out_specs=(pl.BlockSpec((1, c, d), lambda b, i: (b, i, 0)),
                   pl.BlockSpec((1, c, d), lambda b, i: (b, n - 1 - i, 0)), s_blk),
        compiler_params=_params("parallel", "arbitrary"), name="gdn_scan",
    )(u, w, qg, kd, att, dl, u, w, qg, kd, att, dl, s0)
    return o_f.reshape(m, d), o_b.reshape(m, d), s_fin


def _gdn_out_kernel(of_ref, ob_ref, gate_ref, norm_ref, o_ref):
    o = of_ref[...] + ob_ref[...]
    gate = _silu(gate_ref[...].astype(F32))
    parts = []
    for h in range(GDN_HEADS):
        seg = o[:, h * GDN_HEAD_DIM:(h + 1) * GDN_HEAD_DIM]
        parts.append(seg * lax.rsqrt(jnp.mean(seg * seg, axis=-1, keepdims=True) + RMS_EPS) * norm_ref[...])
    o_ref[...] = (jnp.concatenate(parts, axis=-1) * gate).astype(o_ref.dtype)


def _gdn_out(o_f, o_b, gate_lin, out_norm, tm=512):
    m, d = o_f.shape
    tm = min(tm, m)
    blk = pl.BlockSpec((tm, d), lambda i: (i, 0))
    return pl.pallas_call(
        _gdn_out_kernel, out_shape=jax.ShapeDtypeStruct((m, d), BF16), grid=(m // tm,),
        in_specs=[blk, blk, blk, pl.BlockSpec((1, GDN_HEAD_DIM), lambda i: (0, 0))],
        out_specs=blk, compiler_params=_params("parallel"), name="gdn_out",
    )(o_f, o_b, gate_lin, out_norm.reshape(1, GDN_HEAD_DIM))


def _gdn_branch(h, batch, s0, p):
    d = GDN_HEADS * GDN_HEAD_DIM
    lin = _matmul(h, p['w_qkv'])
    u3 = _dwconv3(lin, p['conv'], batch, act=True, l2norm_cols=2 * d, qscale=GDN_HEAD_DIM ** -0.5)
    gcol, grow = _gdn_gates(h, p['w_beta'], p['w_decay'], p['a_log'], p['dt_bias'])
    o_f, o_b, s_fin = _gdn_core(u3, gcol, grow, s0, batch)
    gate_lin = _matmul(h, p['w_gate'])
    return _gdn_out(o_f, o_b, gate_lin, p['out_norm']), s_fin


def _hy_filter_kernel(z_ref, w1_ref, b1_ref, fr_ref, w2_ref, b2_ref, w3f_ref, b3f_ref, w3b_ref, b3b_ref,
                      dl_ref, hf_ref, hb_ref):
    hdot = functools.partial(jnp.dot, preferred_element_type=F32, precision=HIGHEST)
    z = z_ref[...]
    fr = fr_ref[...]
    h1 = jnp.sin(fr[0:1] * (hdot(z, w1_ref[...]) + b1_ref[...]))
    h2 = jnp.sin(fr[1:2] * (hdot(h1, w2_ref[...]) + b2_ref[...]))
    decay = jnp.exp(-z[:, 0:1] * dl_ref[...])
    hf = (hdot(h2, w3f_ref[...]) + b3f_ref[...]) * decay
    hb = (hdot(h2, w3b_ref[...]) + b3b_ref[...]) * decay
    inv = 1.0 / (jnp.sum(jnp.abs(hf), axis=0, keepdims=True) + jnp.sum(jnp.abs(hb), axis=0, keepdims=True))
    hf_ref[0] = hf * inv
    hb_ref[0] = hb * inv


def _hyena_filters(seq, p, tn=256):
    d = D_MODEL
    width = p['w2'].shape[0]
    t = np.linspace(0.0, 1.0, seq)[:, None]
    bands = (HY_EMB_DIM - 1) // 2
    ang = (2.0 * math.pi * np.arange(seq) / seq)[:, None] * np.linspace(1e-4, bands - 1, bands)[None, :]
    feats = np.zeros((seq, LANES), np.float32)
    feats[:, :HY_EMB_DIM] = np.concatenate([t, np.cos(ang), -np.sin(ang)], axis=-1)
    w1 = jnp.zeros((LANES, width), F32).at[:HY_EMB_DIM].set(p['w1'])
    deltas = np.abs(np.linspace(HY_MIN_DECAY, HY_MAX_DECAY, d)).astype(np.float32)[None, :]
    nj = d // tn
    full = lambda shape: pl.BlockSpec(shape, lambda n, j: (0,) * len(shape))
    w3 = lambda dd: pl.BlockSpec((width, tn), lambda n, j: (0, (2 * n + dd) * nj + j))
    b3 = lambda dd: pl.BlockSpec((1, tn), lambda n, j: (0, (2 * n + dd) * nj + j))
    out = jax.ShapeDtypeStruct((HY_ORDER, seq, d), F32)
    oblk = pl.BlockSpec((1, seq, tn), lambda n, j: (n, 0, j))
    b3row = p['b3'].reshape(1, -1)
    return pl.pallas_call(
        _hy_filter_kernel, out_shape=(out, out), grid=(HY_ORDER, nj),
        in_specs=[full((seq, LANES)), full((LANES, width)), full((1, width)), full((2, width)),
                  full((width, width)), full((1, width)), w3(0), b3(0), w3(1), b3(1),
                  pl.BlockSpec((1, tn), lambda n, j: (0, j))],
        out_specs=(oblk, oblk), compiler_params=_params("arbitrary", "arbitrary"), name="hyena_filters",
    )(jnp.asarray(feats), w1, p['b1'].reshape(1, width), p['freq'], p['w2'], p['b2'].reshape(1, width),
      p['w3'], b3row, p['w3'], b3row, jnp.asarray(deltas))


def _dft_matrices(seq):
    f = lax.broadcasted_iota(jnp.int32, (seq, seq), 0)
    t = lax.broadcasted_iota(jnp.int32, (seq, seq), 1)
    ang = ((f * t) % (2 * seq)).astype(F32) * (math.pi / seq)
    nyq = (1 - 2 * (t % 2)).astype(F32)
    fwd = jnp.stack([jnp.cos(ang), jnp.where(f == 0, nyq, -jnp.sin(ang))])
    wgt = jnp.where(lax.broadcasted_iota(jnp.int32, (1, 1, seq), 2) == 0, 0.5 / seq, 1.0 / seq)
    inv = jnp.transpose(fwd, (0, 2, 1)) * wgt
    return fwd.astype(BF16), inv.astype(BF16)


def _hy_spectrum_kernel(f_ref, hf_ref, hb_ref, hr_ref, hi_ref, t_ref):
    hf = hf_ref[0]
    row = lax.broadcasted_iota(jnp.int32, hf.shape, 0)
    hb = jnp.where(row == 0, 0.0, hb_ref[0])
    hs = (hf + hb).astype(BF16)
    hd = (hf - hb).astype(BF16)
    hr = _dot(f_ref[0], hs)
    hi = _dot(f_ref[1], hd)
    nyq = _dot(f_ref[1, 0:8, :], hs)[0:1]
    orow = lax.broadcasted_iota(jnp.int32, hr.shape, 0)
    first = jnp.logical_and(pl.program_id(0) == 0, orow == 0)
    hr_ref[0] = hr
    hi_ref[0] = jnp.where(first, 0.0, hi)
    t_ref[0] = jnp.where(first, nyq, hr)


def _hyena_spectrum(hf, hb, fwd, fm=1024, tn=256):
    n_ord, seq, d = hf.shape
    fm = min(fm, seq)
    out = jax.ShapeDtypeStruct((n_ord, seq, d), F32)
    hblk = pl.BlockSpec((1, seq, tn), lambda c, n, j: (n, 0, j))
    oblk = pl.BlockSpec((1, fm, tn), lambda c, n, j: (n, c, j))
    return pl.pallas_call(
        _hy_spectrum_kernel, out_shape=(out, out, out), grid=(seq // fm, n_ord, d // tn),
        in_specs=[pl.BlockSpec((2, fm, seq), lambda c, n, j: (0, c, 0)), hblk, hblk],
        out_specs=(oblk, oblk, oblk),
        compiler_params=_params("arbitrary", "arbitrary", "arbitrary"), name="hyena_spectrum",
    )(fwd, hf, hb)


def _hy_fwd_kernel(f_ref, z_ref, hr_ref, hi_ref, t_ref, y_ref):
    u = z_ref[0]
    xr = _dot(f_ref[0], u)
    xi = _dot(f_ref[1], u)
    hr, hi, tt = hr_ref[0], hi_ref[0], t_ref[0]
    y_ref[0, 0] = (xr * hr - xi * hi).astype(y_ref.dtype)
    y_ref[0, 1] = (xr * hi + xi * tt).astype(y_ref.dtype)


def _hyena_fwd(z3, z_col0, fwd, hr, hi, tt, order, fm=1024, tn=256):
    batch, seq, _ = z3.shape
    d = D_MODEL
    fm = min(fm, seq)
    zoff = z_col0 // tn
    hblk = pl.BlockSpec((1, fm, tn), lambda c, b, j: (order, c, j))
    return pl.pallas_call(
        _hy_fwd_kernel, out_shape=jax.ShapeDtypeStruct((batch, 2, seq, d), BF16),
        grid=(seq // fm, batch, d // tn),
        in_specs=[pl.BlockSpec((2, fm, seq), lambda c, b, j: (0, c, 0)),
                  pl.BlockSpec((1, seq, tn), lambda c, b, j: (b, 0, zoff + j)), hblk, hblk, hblk],
        out_specs=pl.BlockSpec((1, 2, fm, tn), lambda c, b, j: (b, 0, c, j)),
        compiler_params=_params("arbitrary", "arbitrary", "arbitrary"), name="hyena_fwd_dft",
    )(fwd, z3, hr, hi, tt)


def _hy_inv_kernel(g_ref, y_ref, z_ref, gate_ref, skip_ref, o_ref):
    y = _dot(g_ref[0], y_ref[0, 0]) + _dot(g_ref[1], y_ref[0, 1])
    conv = y + z_ref[0].astype(F32) * skip_ref[...]
    o_ref[0] = (gate_ref[0].astype(F32) * conv).astype(o_ref.dtype)


def _hyena_inv(y, inv, z3, z_col0, gate3, gate_col0, skip, tmc=1024, tn=256):
    batch, _, seq, d = y.shape
    tmc = min(tmc, seq)
    zoff, goff = z_col0 // tn, gate_col0 // tn
    return pl.pallas_call(
        _hy_inv_kernel, out_shape=jax.ShapeDtypeStruct((batch, seq, d), BF16),
        grid=(seq // tmc, batch, d // tn),
        in_specs=[pl.BlockSpec((2, tmc, seq), lambda c, b, j: (0, c, 0)),
                  pl.BlockSpec((1, 2, seq, tn), lambda c, b, j: (b, 0, 0, j)),
                  pl.BlockSpec((1, tmc, tn), lambda c, b, j: (b, c, zoff + j)),
                  pl.BlockSpec((1, tmc, tn), lambda c, b, j: (b, c, goff + j)),
                  pl.BlockSpec((1, tn), lambda c, b, j: (0, j))],
        out_specs=pl.BlockSpec((1, tmc, tn), lambda c, b, j: (b, c, j)),
        compiler_params=_params("arbitrary", "arbitrary", "arbitrary"), name="hyena_inv_dft",
    )(inv, y, z3, gate3, skip.reshape(1, d))


def _hyena_branch(h, batch, p):
    d = D_MODEL
    m = h.shape[0]
    seq = m // batch
    xs = _dwconv3(_matmul(h, p['w_in']), p['conv'], batch).reshape(batch, seq, 3 * d)
    hf, hb = _hyena_filters(seq, p)
    fwd, inv = _dft_matrices(seq)
    hr, hi, tt = _hyena_spectrum(hf, hb, fwd)
    z, z_col0 = xs, 2 * d
    for n in range(HY_ORDER):
        y = _hyena_fwd(z, z_col0, fwd, hr, hi, tt, n)
        z, z_col0 = _hyena_inv(y, inv, z, z_col0, xs, n * d, p['skip'][n]), 0
    return z.reshape(m, d)


DENSE_FFN_PARTS = 2


def kernel(x, c, ctx, c_ctx, ada_w, ada_b, norm_g, na_w_qkv, na_q_norm, na_k_norm, na_rpb, na_w_o, gdn_w_qkv, gdn_conv, gdn_w_gate, gdn_w_beta, gdn_w_decay, gdn_a_log, gdn_dt_bias, gdn_out_norm, gdn_w_o, diff_w_qkv, diff_q_norm, diff_k_norm, diff_lambda, diff_out_norm, diff_w_o, hy_w_in, hy_conv, hy_filt_w1, hy_filt_b1, hy_filt_freq, hy_filt_w2, hy_filt_b2, hy_filt_w3, hy_filt_b3, hy_skip, hy_w_o, ffn_w_in, ffn_w_out, moe_router, moe_w_in, moe_w_out):
    batch, seq, d = x.shape
    n_ctx = ctx.shape[1]
    depth = ada_w.shape[0]
    bf = lambda w: w.astype(BF16)

    n_rows = -(-(batch + 1) // 8) * 8
    cc = jnp.zeros((n_rows, d), F32).at[:batch].set(c).at[batch].set(c_ctx)
    mods = _ada_mods(cc, ada_w, ada_b)

    xl = x.reshape(batch * seq, d)
    xc = ctx.reshape(batch * n_ctx, d)
    for i in range(depth):
        last = i == depth - 1
        kind = i % 4
        ml = [mods[i, :batch, k * d:(k + 1) * d].reshape(batch, 1, d) for k in range(6)]
        mc = [mods[i, batch:batch + 1, k * d:(k + 1) * d].reshape(1, 1, d) for k in range(6)]
        ctx_needed = (not last) or kind != 3
        hl = _norm_modulate(xl, norm_g[i, 0], ml[0], ml[1])
        hc = _norm_modulate(xc, norm_g[i, 0], mc[0], mc[1]) if ctx_needed else None
        oc = None
        if kind == 0:
            w = bf(na_w_qkv)
            qkv, qkv_c = _matmul(hl, w), _matmul(hc, w)
            gq = jnp.tile(na_q_norm, NA_HEADS) * NA_HEAD_DIM ** -0.5
            gk = jnp.tile(na_k_norm, NA_HEADS)
            ol, oc = _neighbourhood_attention(
                _head_norm(qkv, 0, gq, NA_HEAD_DIM), _head_norm(qkv, 1, gk, NA_HEAD_DIM), qkv,
                _head_norm(qkv_c, 0, gq, NA_HEAD_DIM), _head_norm(qkv_c, 1, gk, NA_HEAD_DIM), qkv_c,
                na_rpb, batch)
            w_o = bf(na_w_o)
        elif kind == 1:
            p = dict(w_qkv=bf(gdn_w_qkv), conv=gdn_conv, w_gate=bf(gdn_w_gate), w_beta=gdn_w_beta,
                     w_decay=gdn_w_decay, a_log=gdn_a_log, dt_bias=gdn_dt_bias, out_norm=gdn_out_norm)
            zeros = jnp.zeros((batch, 2, GDN_HEADS, GDN_HEAD_DIM, GDN_HEAD_DIM), F32)
            oc, s_ctx = _gdn_branch(hc, batch, zeros, p)
            ol, _ = _gdn_branch(hl, batch, s_ctx, p)
            w_o = bf(gdn_w_o)
        elif kind == 2:
            lambda_init = 0.8 - 0.6 * math.exp(-0.3 * i)
            w = bf(diff_w_qkv)
            qkv, qkv_c = _matmul(hl, w), _matmul(hc, w)
            reps = d // DIFF_HEAD_DIM
            gq = jnp.tile(diff_q_norm, reps) * DIFF_HEAD_DIM ** -0.5
            gk = jnp.tile(diff_k_norm, reps)
            ol, oc = _diff_attention(
                _head_norm(qkv, 0, gq, DIFF_HEAD_DIM, seq=seq, rope=True),
                _head_norm(qkv, 1, gk, DIFF_HEAD_DIM, seq=seq, rope=True), qkv,
                _head_norm(qkv_c, 0, gq, DIFF_HEAD_DIM), _head_norm(qkv_c, 1, gk, DIFF_HEAD_DIM), qkv_c,
                diff_lambda, diff_out_norm, lambda_init, batch)
            w_o = bf(diff_w_o)
        else:
            p = dict(w_in=bf(hy_w_in), conv=hy_conv, w1=hy_filt_w1, b1=hy_filt_b1, freq=hy_filt_freq,
                     w2=hy_filt_w2, b2=hy_filt_b2, w3=hy_filt_w3, b3=hy_filt_b3, skip=hy_skip)
            ol = _hyena_branch(hl, batch, p)
            oc = _hyena_branch(hc, batch, p) if ctx_needed else None
            w_o = bf(hy_w_o)
        xl = _matmul_residual(ol, w_o, xl, ml[2])
        if not last:
            xc = _matmul_residual(oc, w_o, xc, mc[2])

        j = i // 2
        streams = [(xl, ml)] if last else [(xl, ml), (xc, mc)]
        outs = []
        for xs, ms in streams:
            if i % 2 == 0:
                hs = _norm_modulate(xs, norm_g[i, 1], ms[3], ms[4])
                w_in = bf(ffn_w_in[j]).reshape(1, d, -1)
                w_out = bf(ffn_w_out[j]).reshape(DENSE_FFN_PARTS, -1, d)
                outs.append(_ffn(hs, w_in, w_out, xs, ms[5], dense_parts=DENSE_FFN_PARTS))
            else:
                hs, gates = _norm_modulate(xs, norm_g[i, 1], ms[3], ms[4], router=moe_router[j])
                outs.append(_ffn(hs, bf(moe_w_in[j]), bf(moe_w_out[j]), xs, ms[5], gates=gates))
        xl = outs[0]
        if not last:
            xc = outs[1]
    return xl.reshape(batch, seq, d)
```

```python
import functools
import math

import numpy as np
import jax
import jax.numpy as jnp
from jax import lax
from jax.experimental import pallas as pl
from jax.experimental.pallas import tpu as pltpu

F32 = jnp.float32
BF16 = jnp.bfloat16
HIGHEST = lax.Precision.HIGHEST

VMEM_LIMIT_BYTES = 56 * 1024 * 1024
LANES = 128

D_MODEL = 1024
GRID_W = 64
RMS_EPS = 1e-6
L2_EPS = 1e-6
ROPE_BASE = 10000.0
NA_HEADS = 16
NA_HEAD_DIM = 64
NA_WIN_ROWS = 8
NA_WIN_COLS = 16
GDN_HEADS = 8
GDN_HEAD_DIM = 128
GDN_CHUNK = 64
GDN_SOLVE_BASE = 8
DIFF_HEADS = 8
DIFF_HEAD_DIM = 64
HY_ORDER = 2
HY_EMB_DIM = 33
HY_MAX_DECAY = math.log(1e-2) / 0.3
HY_MIN_DECAY = math.log(1e-2) / 1.5
MOE_EXPERTS = 8
NEG_BIG = -1e30


def _params(*sem):
    return pltpu.CompilerParams(dimension_semantics=sem, vmem_limit_bytes=VMEM_LIMIT_BYTES)


def _dot(a, b):
    return jnp.dot(a, b, preferred_element_type=F32)


def _dot_t(a, b):
    return lax.dot_general(a, b, (((1,), (1,)), ((), ())), preferred_element_type=F32)


def _silu(x):
    return x * (1.0 / (1.0 + jnp.exp(-x)))


def _ada_kernel(c_ref, w_ref, b_ref, o_ref):
    s = _silu(c_ref[...])
    o_ref[0] = jnp.dot(s, w_ref[0], preferred_element_type=F32, precision=HIGHEST) + b_ref[0]


def _ada_mods(cc, ada_w, ada_b):
    depth, d, n = ada_w.shape
    r = cc.shape[0]
    tn = 1024
    return pl.pallas_call(
        _ada_kernel,
        out_shape=jax.ShapeDtypeStruct((depth, r, n), F32),
        grid=(depth, n // tn),
        in_specs=[pl.BlockSpec((r, d), lambda i, j: (0, 0)),
                  pl.BlockSpec((1, d, tn), lambda i, j: (i, 0, j)),
                  pl.BlockSpec((1, 1, tn), lambda i, j: (i, 0, j))],
        out_specs=pl.BlockSpec((1, r, tn), lambda i, j: (i, 0, j)),
        compiler_params=_params("arbitrary", "arbitrary"),
        name="ada_mods",
    )(cc, ada_w, ada_b.reshape(depth, 1, n))


def _norm_mod(x_ref, g_ref, sh_ref, sc_ref):
    x = x_ref[...]
    y = x * lax.rsqrt(jnp.mean(x * x, axis=-1, keepdims=True) + RMS_EPS) * g_ref[...]
    return y * (1.0 + sc_ref[0]) + sh_ref[0]


def _norm_mod_kernel(x_ref, g_ref, sh_ref, sc_ref, o_ref):
    o_ref[...] = _norm_mod(x_ref, g_ref, sh_ref, sc_ref).astype(o_ref.dtype)


def _norm_mod_route_kernel(x_ref, g_ref, sh_ref, sc_ref, r_ref, o_ref, info_ref, cnt_ref):
    h = _norm_mod(x_ref, g_ref, sh_ref, sc_ref)
    o_ref[...] = h
    logits = jnp.dot(h, r_ref[...], preferred_element_type=F32, precision=HIGHEST)
    lane = lax.broadcasted_iota(jnp.int32, logits.shape, 1).astype(F32)
    l1 = jnp.where(lane < MOE_EXPERTS, logits, NEG_BIG)
    m1 = jnp.max(l1, axis=-1, keepdims=True)
    i1 = jnp.min(jnp.where(l1 == m1, lane, float(LANES)), axis=-1, keepdims=True)
    l2 = jnp.where(lane == i1, NEG_BIG, l1)
    m2 = jnp.max(l2, axis=-1, keepdims=True)
    i2 = jnp.min(jnp.where(l2 == m2, lane, float(LANES)), axis=-1, keepdims=True)
    e2 = jnp.exp(m2 - m1)
    w1 = 1.0 / (1.0 + e2)
    info_ref[...] = (jnp.where(lane == 0.0, i1, 0.0) + jnp.where(lane == 1.0, i2, 0.0)
                     + jnp.where(lane == 2.0, w1, 0.0) + jnp.where(lane == 3.0, e2 * w1, 0.0))
    hit = jnp.logical_or(lane == i1, lane == i2).astype(F32)
    cnt_ref[0] = jnp.sum(hit, axis=0, keepdims=True)


def _norm_modulate(x, g, shift, scale, router=None, tm=512):
    m, d = x.shape
    grp = shift.shape[0]
    rows = m // grp
    tm = min(tm, rows)
    assert rows % tm == 0
    in_specs = [pl.BlockSpec((tm, d), lambda i: (i, 0)),
                pl.BlockSpec((1, d), lambda i: (0, 0)),
                pl.BlockSpec((1, 1, d), lambda i: (i * tm // rows, 0, 0)),
                pl.BlockSpec((1, 1, d), lambda i: (i * tm // rows, 0, 0))]
    args = [x, g.reshape(1, d), shift, scale]
    if router is None:
        return pl.pallas_call(
            _norm_mod_kernel, out_shape=jax.ShapeDtypeStruct((m, d), BF16), grid=(m // tm,),
            in_specs=in_specs, out_specs=pl.BlockSpec((tm, d), lambda i: (i, 0)),
            compiler_params=_params("parallel"), name="norm_mod")(*args)
    rpad = jnp.zeros((d, LANES), F32).at[:, :router.shape[1]].set(router)
    assert tm == ROUTE_TILE
    return pl.pallas_call(
        _norm_mod_route_kernel,
        out_shape=(jax.ShapeDtypeStruct((m, d), F32), jax.ShapeDtypeStruct((m, LANES), F32),
                   jax.ShapeDtypeStruct((m // tm, 1, LANES), F32)),
        grid=(m // tm,),
        in_specs=in_specs + [pl.BlockSpec((d, LANES), lambda i: (0, 0))],
        out_specs=(pl.BlockSpec((tm, d), lambda i: (i, 0)), pl.BlockSpec((tm, LANES), lambda i: (i, 0)),
                   pl.BlockSpec((1, 1, LANES), lambda i: (i, 0, 0))),
        compiler_params=_params("parallel"), name="norm_mod_route")(*args, rpad)


def _mm_kernel(x_ref, w_ref, o_ref):
    o_ref[...] = _dot(x_ref[...], w_ref[...]).astype(o_ref.dtype)


def _mm_res_kernel(x_ref, w_ref, res_ref, gate_ref, o_ref):
    o_ref[...] = res_ref[...] + gate_ref[0] * _dot(x_ref[...], w_ref[...])


def _matmul(x, w, out_dtype=BF16, tm=1024, tn=1024):
    m, k = x.shape
    n = w.shape[1]
    tm, tn = min(tm, m), min(tn, n)
    assert m % tm == 0 and n % tn == 0
    return pl.pallas_call(
        _mm_kernel, out_shape=jax.ShapeDtypeStruct((m, n), out_dtype), grid=(m // tm, n // tn),
        in_specs=[pl.BlockSpec((tm, k), lambda i, j: (i, 0)), pl.BlockSpec((k, tn), lambda i, j: (0, j))],
        out_specs=pl.BlockSpec((tm, tn), lambda i, j: (i, j)),
        compiler_params=_params("parallel", "arbitrary"), name="matmul")(x, w)


def _matmul_residual(x, w, res, gate, tm=1024, tn=1024):
    m, k = x.shape
    n = w.shape[1]
    rows = m // gate.shape[0]
    tm, tn = min(tm, rows), min(tn, n)
    assert rows % tm == 0 and n % tn == 0
    return pl.pallas_call(
        _mm_res_kernel, out_shape=jax.ShapeDtypeStruct((m, n), F32), grid=(m // tm, n // tn),
        in_specs=[pl.BlockSpec((tm, k), lambda i, j: (i, 0)), pl.BlockSpec((k, tn), lambda i, j: (0, j)),
                  pl.BlockSpec((tm, tn), lambda i, j: (i, j)),
                  pl.BlockSpec((1, 1, tn), lambda i, j: (i * tm // rows, 0, j))],
        out_specs=pl.BlockSpec((tm, tn), lambda i, j: (i, j)),
        compiler_params=_params("parallel", "arbitrary"), name="matmul_residual")(x, w, res, gate)


def _swiglu(x, wa_ref, wb_ref, wo_ref):
    a = _dot(x, wa_ref[0])
    b = _dot(x, wb_ref[0])
    return _dot((_silu(a) * b).astype(BF16), wo_ref[0])


def _ffn_kernel(h_ref, wa_ref, wb_ref, wo_ref, res_ref, mod_ref, o_ref, acc_ref):
    e = pl.program_id(1)
    y = _swiglu(h_ref[...], wa_ref, wb_ref, wo_ref)

    @pl.when(e == 0)
    def _():
        acc_ref[...] = y

    @pl.when(e > 0)
    def _():
        acc_ref[...] += y

    @pl.when(e == pl.num_programs(1) - 1)
    def _():
        o_ref[...] = res_ref[...] + mod_ref[0] * acc_ref[...]


def _dense_ffn(h, w_in, w_out, res, mod, tm=512):
    m, d = h.shape
    rows = m // mod.shape[0]
    tm = min(tm, rows)
    assert rows % tm == 0
    parts, f, _ = w_out.shape
    return pl.pallas_call(
        _ffn_kernel, out_shape=jax.ShapeDtypeStruct((m, d), F32), grid=(m // tm, parts),
        in_specs=[pl.BlockSpec((tm, d), lambda i, e: (i, 0)),
                  pl.BlockSpec((1, d, f), lambda i, e: (0, 0, e)),
                  pl.BlockSpec((1, d, f), lambda i, e: (0, 0, parts + e)),
                  pl.BlockSpec((1, f, d), lambda i, e: (e, 0, 0)),
                  pl.BlockSpec((tm, d), lambda i, e: (i, 0)),
                  pl.BlockSpec((1, 1, d), lambda i, e: (i * tm // rows, 0, 0))],
        out_specs=pl.BlockSpec((tm, d), lambda i, e: (i, 0)),
        scratch_shapes=[pltpu.VMEM((tm, d), F32)],
        compiler_params=_params("parallel", "arbitrary"), name="ffn")(h, w_in, w_in, w_out, res, mod)


ROUTE_TILE = 512
EXPERT_TILE = 512
DMA_CHUNK = 256


def _route_plan_kernel(cnt_ref, base_ref, te_ref, used_ref):
    hdot = functools.partial(jnp.dot, preferred_element_type=F32, precision=HIGHEST)
    cnt = cnt_ref[...]
    nt = cnt.shape[0]
    tot = jnp.sum(cnt, axis=0, keepdims=True)
    seg = jnp.floor((tot + (EXPERT_TILE - 1)) * (1.0 / EXPERT_TILE)) * EXPERT_TILE
    e_r = lax.broadcasted_iota(jnp.int32, (LANES, LANES), 0)
    e_c = lax.broadcasted_iota(jnp.int32, (LANES, LANES), 1)
    off = hdot(jnp.broadcast_to(seg, (8, LANES)), (e_r < e_c).astype(F32))[0:1]
    t_r = lax.broadcasted_iota(jnp.int32, (nt, nt), 0)
    t_c = lax.broadcasted_iota(jnp.int32, (nt, nt), 1)
    base_ref[...] = hdot((t_c < t_r).astype(F32), cnt) + off
    ends = off + seg
    start = lax.broadcasted_iota(jnp.int32, te_ref.shape, 0).astype(F32) * EXPERT_TILE
    lane = lax.broadcasted_iota(jnp.int32, te_ref.shape, 1)
    done = jnp.where(jnp.logical_and(lane < MOE_EXPERTS, start >= ends), 1.0, 0.0)
    te = jnp.minimum(jnp.sum(done, axis=-1, keepdims=True), MOE_EXPERTS - 1.0)
    te_ref[...] = jnp.broadcast_to(te, te_ref.shape).astype(jnp.int32)
    used = ends[:, MOE_EXPERTS - 1:MOE_EXPERTS] * (1.0 / EXPERT_TILE)
    used_ref[...] = jnp.broadcast_to(used, used_ref.shape).astype(jnp.int32)


def _route_pos_kernel(info_ref, base_ref, pos_ref):
    info = info_ref[...]
    lane = lax.broadcasted_iota(jnp.int32, info.shape, 1).astype(F32)
    hit1 = lane == info[:, 0:1]
    hit2 = lane == info[:, 1:2]
    tm = info.shape[0]
    r_i = lax.broadcasted_iota(jnp.int32, (tm, tm), 0)
    r_j = lax.broadcasted_iota(jnp.int32, (tm, tm), 1)
    earlier = jnp.where(r_j < r_i, 1.0, 0.0).astype(BF16)
    rank = _dot(earlier, jnp.where(jnp.logical_or(hit1, hit2), 1.0, 0.0).astype(BF16))
    p = rank + base_ref[0]
    pos1 = jnp.sum(jnp.where(hit1, p, 0.0), axis=-1, keepdims=True)
    pos2 = jnp.sum(jnp.where(hit2, p, 0.0), axis=-1, keepdims=True)
    pos_ref[...] = jnp.where(lane == 0.0, pos1, jnp.where(lane == 1.0, pos2, 0.0)).astype(jnp.int32)


def _row_copies(n_tokens, chunk, make_copy):
    first = pl.program_id(0) * chunk

    def start(j, carry):
        for k in range(2):
            make_copy(first + j, j, k).start()
        return carry

    def wait(j, carry):
        for k in range(2):
            make_copy(first + j, j, k).wait()
        return carry

    lax.fori_loop(0, chunk, start, 0, unroll=8)
    lax.fori_loop(0, chunk, wait, 0, unroll=8)


def _dispatch_kernel(pos_ref, h_ref, _, xs_ref, sem, *, n_tokens):
    def make_copy(t, j, k):
        return pltpu.make_async_copy(h_ref.at[pl.ds(t, 1)], xs_ref.at[pl.ds(pos_ref[k * n_tokens + t], 1)], sem)

    _row_copies(n_tokens, DMA_CHUNK, make_copy)


def _ffn_routed_kernel(te_ref, used_ref, x_ref, wa_ref, wb_ref, wo_ref, y_ref):
    live = pl.program_id(0) < used_ref[0]

    @pl.when(live)
    def _():
        y_ref[...] = _swiglu(x_ref[...].astype(BF16), wa_ref, wb_ref, wo_ref)

    @pl.when(jnp.logical_not(live))
    def _():
        y_ref[...] = jnp.zeros_like(y_ref)


def _combine_kernel(pos_ref, ys_ref, res_ref, info_ref, mod_ref, o_ref, buf, sem, *, n_tokens):
    def make_copy(t, j, k):
        return pltpu.make_async_copy(ys_ref.at[pl.ds(pos_ref[k * n_tokens + t], 1)], buf.at[k, pl.ds(j, 1)], sem)

    _row_copies(n_tokens, DMA_CHUNK, make_copy)
    info = info_ref[...]
    y = info[:, 2:3] * buf[0] + info[:, 3:4] * buf[1]
    o_ref[...] = res_ref[...] + mod_ref[0] * y


def _moe(x, g, shift, scale, mod, router, w_in, w_out):
    m, d = x.shape
    n_e, _, two_f = w_in.shape
    f = two_f // 2
    rows = m // mod.shape[0]
    nt = m // ROUTE_TILE
    n_tiles = 2 * m // EXPERT_TILE + n_e
    n_sorted = n_tiles * EXPERT_TILE
    h32, info, cnt = _norm_modulate(x, g, shift, scale, router=router, tm=ROUTE_TILE)

    te_rows = -(-n_tiles // 8) * 8
    base, te, used = pl.pallas_call(
        _route_plan_kernel,
        out_shape=(jax.ShapeDtypeStruct((nt, LANES), F32), jax.ShapeDtypeStruct((te_rows, LANES), jnp.int32),
                   jax.ShapeDtypeStruct((8, LANES), jnp.int32)),
        name="route_plan")(cnt.reshape(nt, LANES))
    pos = pl.pallas_call(
        _route_pos_kernel, out_shape=jax.ShapeDtypeStruct((m, LANES), jnp.int32), grid=(nt,),
        in_specs=[pl.BlockSpec((ROUTE_TILE, LANES), lambda i: (i, 0)),
                  pl.BlockSpec((1, 1, LANES), lambda i: (i, 0, 0))],
        out_specs=pl.BlockSpec((ROUTE_TILE, LANES), lambda i: (i, 0)),
        compiler_params=_params("parallel"), name="route_pos")(info, base.reshape(nt, 1, LANES))
    pos_flat = jnp.concatenate([pos[:, 0], pos[:, 1]])
    te_flat, used_flat = te[:, 0], used[0, :1]

    any_spec = pl.BlockSpec(memory_space=pl.ANY)
    xs = pl.pallas_call(
        functools.partial(_dispatch_kernel, n_tokens=m),
        grid_spec=pltpu.PrefetchScalarGridSpec(
            num_scalar_prefetch=1, grid=(m // DMA_CHUNK,), in_specs=[any_spec, any_spec], out_specs=any_spec,
            scratch_shapes=[pltpu.SemaphoreType.DMA(())]),
        out_shape=jax.ShapeDtypeStruct((n_sorted, d), F32), input_output_aliases={2: 0},
        compiler_params=_params("arbitrary"), name="moe_dispatch",
    )(pos_flat, h32, jnp.zeros((n_sorted, d), F32))

    ys = pl.pallas_call(
        _ffn_routed_kernel,
        grid_spec=pltpu.PrefetchScalarGridSpec(
            num_scalar_prefetch=2, grid=(n_tiles,),
            in_specs=[pl.BlockSpec((EXPERT_TILE, d), lambda i, te, used: (i, 0)),
                      pl.BlockSpec((1, d, f), lambda i, te, used: (te[i], 0, 0)),
                      pl.BlockSpec((1, d, f), lambda i, te, used: (te[i], 0, 1)),
                      pl.BlockSpec((1, f, d), lambda i, te, used: (te[i], 0, 0))],
            out_specs=pl.BlockSpec((EXPERT_TILE, d), lambda i, te, used: (i, 0))),
        out_shape=jax.ShapeDtypeStruct((n_sorted, d), F32),
        compiler_params=_params("arbitrary"), name="moe_expert_ffn",
    )(te_flat, used_flat, xs, w_in, w_in, w_out)

    return pl.pallas_call(
        functools.partial(_combine_kernel, n_tokens=m),
        grid_spec=pltpu.PrefetchScalarGridSpec(
            num_scalar_prefetch=1, grid=(m // DMA_CHUNK,),
            in_specs=[any_spec,
                      pl.BlockSpec((DMA_CHUNK, d), lambda i, pos: (i, 0)),
                      pl.BlockSpec((DMA_CHUNK, LANES), lambda i, pos: (i, 0)),
                      pl.BlockSpec((1, 1, d), lambda i, pos: (i * DMA_CHUNK // rows, 0, 0))],
            out_specs=pl.BlockSpec((DMA_CHUNK, d), lambda i, pos: (i, 0)),
            scratch_shapes=[pltpu.VMEM((2, DMA_CHUNK, d), F32), pltpu.SemaphoreType.DMA(())]),
        out_shape=jax.ShapeDtypeStruct((m, d), F32),
        compiler_params=_params("arbitrary"), name="moe_combine",
    )(pos_flat, ys, x, info, mod)


def _block_diag_ones(d, group):
    idx = np.arange(d) // group
    return jnp.asarray((idx[:, None] == idx[None, :]).astype(np.float32)).astype(BF16)


def _head_norm_kernel(*refs, group, rope):
    if rope:
        x_ref, gain_ref, bd_ref, cos_ref, sa_ref, sb_ref, o_ref = refs
    else:
        x_ref, gain_ref, bd_ref, o_ref = refs
    x = x_ref[...].astype(F32)
    ss = _dot((x * x).astype(BF16), bd_ref[...])
    y = x * lax.rsqrt(ss * (1.0 / group) + RMS_EPS) * gain_ref[...]
    if rope:
        d = y.shape[-1]
        rep = d // LANES
        half = group // 4
        y = (y * pltpu.repeat(cos_ref[...], rep, axis=1)
             + pltpu.roll(y, d - half, 1) * pltpu.repeat(sa_ref[...], rep, axis=1)
             + pltpu.roll(y, half, 1) * pltpu.repeat(sb_ref[...], rep, axis=1))
    o_ref[...] = y.astype(o_ref.dtype)


def _rope_tables(seq):
    pos = np.arange(seq)
    lane = np.arange(LANES) % DIFF_HEAD_DIM
    half = DIFF_HEAD_DIM // 2
    j = lane % half
    inv = ROPE_BASE ** (-(2.0 * (j % (half // 2))) / half)
    p = np.where(lane[None, :] < half, (pos // GRID_W)[:, None], (pos % GRID_W)[:, None])
    ang = p * inv[None, :]
    first = (j < half // 2)[None, :]
    cos, sin = np.cos(ang), np.sin(ang)
    return (jnp.asarray(cos, F32), jnp.asarray(np.where(first, -sin, 0.0), F32),
            jnp.asarray(np.where(first, 0.0, sin), F32))


def _head_norm(src, col_block, gain, group, seq=None, rope=False, tm=512):
    m = src.shape[0]
    d = D_MODEL
    tm = min(tm, m if seq is None else seq)
    assert m % tm == 0
    in_specs = [pl.BlockSpec((tm, d), lambda i: (i, col_block)),
                pl.BlockSpec((1, d), lambda i: (0, 0)),
                pl.BlockSpec((d, d), lambda i: (0, 0))]
    args = [src, gain.reshape(1, d), _block_diag_ones(d, group)]
    if rope:
        nblk = seq // tm
        in_specs += [pl.BlockSpec((tm, LANES), lambda i: (i % nblk, 0))] * 3
        args += list(_rope_tables(seq))
    return pl.pallas_call(
        functools.partial(_head_norm_kernel, group=group, rope=rope),
        out_shape=jax.ShapeDtypeStruct((m, d), BF16), grid=(m // tm,),
        in_specs=in_specs, out_specs=pl.BlockSpec((tm, d), lambda i: (i, 0)),
        compiler_params=_params("parallel"), name="head_norm")(*args)


def _na_bias_table(rpb, rows):
    wr = min(NA_WIN_ROWS, rows)
    cols = np.arange(GRID_W)
    c0 = np.clip(cols - NA_WIN_COLS // 2, 0, GRID_W - NA_WIN_COLS)
    col_valid = (cols[None, :] >= c0[:, None]) & (cols[None, :] < c0[:, None] + NA_WIN_COLS)
    col_idx = np.clip(cols[None, :] - cols[:, None], 1 - NA_WIN_COLS, NA_WIN_COLS - 1) + NA_WIN_COLS - 1
    pick = jnp.asarray((col_idx[None] == np.arange(2 * NA_WIN_COLS - 1)[:, None, None]).astype(np.float32))
    lo = NA_WIN_ROWS - 1
    rows_of = jnp.stack([rpb[:, lo - off:lo - off + wr] for off in range(wr)])
    bias = jnp.einsum('ohic,cqk->ohqik', rows_of, pick, precision=HIGHEST)
    bias = jnp.where(col_valid[None, None, :, None, :], bias, NEG_BIG)
    return bias.reshape(wr, NA_HEADS, GRID_W, wr * GRID_W).astype(F32)


def _na_row_start(r, rows, wr):
    return jnp.clip(r - wr // 2, 0, rows - wr)


def _na_kernel(q_ref, k_ref, v_ref, kc_ref, vc_ref, bias_ref, o_ref, *, rows, wr):
    r = pl.program_id(1)
    start = pl.multiple_of(_na_row_start(r, rows, wr) * GRID_W, GRID_W)
    lane = lax.broadcasted_iota(jnp.int32, (GRID_W, LANES), 1)
    low = lane < NA_HEAD_DIM
    win = pl.ds(start, wr * GRID_W)
    cols = lambda h: slice((h // 2) * LANES, (h // 2 + 1) * LANES)
    heads = range(NA_HEADS)
    scores = []
    for h in heads:
        q = q_ref[0, :, cols(h)]
        qm = jnp.where(low if h % 2 == 0 else ~low, q, jnp.zeros_like(q))
        scores.append((_dot_t(qm, k_ref[0, win, cols(h)]) + bias_ref[0, h], _dot_t(qm, kc_ref[0, :, cols(h)])))
    probs = []
    for s_loc, s_ctx in scores:
        m = jnp.maximum(jnp.max(s_loc, axis=-1, keepdims=True), jnp.max(s_ctx, axis=-1, keepdims=True))
        p_loc = jnp.exp(s_loc - m)
        p_ctx = jnp.exp(s_ctx - m)
        z = jnp.sum(p_loc, axis=-1, keepdims=True) + jnp.sum(p_ctx, axis=-1, keepdims=True)
        probs.append((p_loc.astype(BF16), p_ctx.astype(BF16), 1.0 / z))
    outs = [(_dot(p_loc, v_ref[0, win, cols(h)]) + _dot(p_ctx, vc_ref[0, :, cols(h)])) * rz
            for h, (p_loc, p_ctx, rz) in zip(heads, probs)]
    for pair in range(NA_HEADS // 2):
        o_ref[0, :, cols(2 * pair)] = jnp.where(low, outs[2 * pair], outs[2 * pair + 1]).astype(o_ref.dtype)


def _na_ctx_kernel(q_ref, k_ref, v_ref, o_ref):
    lane = lax.broadcasted_iota(jnp.int32, (q_ref.shape[1], LANES), 1)
    low = lane < NA_HEAD_DIM
    for pair in range(NA_HEADS // 2):
        cs = slice(pair * LANES, (pair + 1) * LANES)
        q, k, v = q_ref[0, :, cs], k_ref[0, :, cs], v_ref[0, :, cs]
        outs = []
        for sub in range(2):
            qm = jnp.where(low if sub == 0 else ~low, q, jnp.zeros_like(q))
            s = _dot_t(qm, k)
            p = jnp.exp(s - jnp.max(s, axis=-1, keepdims=True))
            outs.append(_dot(p.astype(BF16), v) * (1.0 / jnp.sum(p, axis=-1, keepdims=True)))
        o_ref[0, :, cs] = jnp.where(low, outs[0], outs[1]).astype(o_ref.dtype)


def _neighbourhood_attention(q, k, qkv, qc, kc, qkv_c, rpb, batch):
    d = D_MODEL
    s = q.shape[0] // batch
    n_ctx = qc.shape[0] // batch
    rows = s // GRID_W
    wr = min(NA_WIN_ROWS, rows)
    bias = _na_bias_table(rpb, rows)
    q3, k3, qkv3 = q.reshape(batch, s, d), k.reshape(batch, s, d), qkv.reshape(batch, s, 3 * d)
    qc3, kc3, qkvc3 = qc.reshape(batch, n_ctx, d), kc.reshape(batch, n_ctx, d), qkv_c.reshape(batch, n_ctx, 3 * d)

    def variant(b, r):
        return (r - _na_row_start(r, rows, wr), 0, 0, 0)

    ol = pl.pallas_call(
        functools.partial(_na_kernel, rows=rows, wr=wr),
        out_shape=jax.ShapeDtypeStruct((batch, s, d), BF16), grid=(batch, rows),
        in_specs=[pl.BlockSpec((1, GRID_W, d), lambda b, r: (b, r, 0)),
                  pl.BlockSpec((1, s, d), lambda b, r: (b, 0, 0)),
                  pl.BlockSpec((1, s, d), lambda b, r: (b, 0, 2)),
                  pl.BlockSpec((1, n_ctx, d), lambda b, r: (b, 0, 0)),
                  pl.BlockSpec((1, n_ctx, d), lambda b, r: (b, 0, 2)),
                  pl.BlockSpec((1, NA_HEADS, GRID_W, wr * GRID_W), variant)],
        out_specs=pl.BlockSpec((1, GRID_W, d), lambda b, r: (b, r, 0)),
        compiler_params=_params("parallel", "arbitrary"), name="na_attention",
    )(q3, k3, qkv3, kc3, qkvc3, bias)
    oc = pl.pallas_call(
        _na_ctx_kernel, out_shape=jax.ShapeDtypeStruct((batch, n_ctx, d), BF16), grid=(batch,),
        in_specs=[pl.BlockSpec((1, n_ctx, d), lambda b: (b, 0, 0)),
                  pl.BlockSpec((1, n_ctx, d), lambda b: (b, 0, 0)),
                  pl.BlockSpec((1, n_ctx, d), lambda b: (b, 0, 2))],
        out_specs=pl.BlockSpec((1, n_ctx, d), lambda b: (b, 0, 0)),
        compiler_params=_params("parallel"), name="na_ctx_attention",
    )(qc3, kc3, qkvc3)
    return ol.reshape(batch * s, d), oc.reshape(batch * n_ctx, d)


def _diff_lambda(lam_ref, lambda_init):
    lv = lam_ref[...]
    a = jnp.sum(lv[0:1] * lv[1:2], axis=-1, keepdims=True)
    b = jnp.sum(lv[2:3] * lv[3:4], axis=-1, keepdims=True)
    return jnp.exp(a) - jnp.exp(b) + lambda_init


def _diff_out(o, onorm_ref, lambda_init, o_ref):
    y = o * lax.rsqrt(jnp.mean(o * o, axis=-1, keepdims=True) + RMS_EPS) * onorm_ref[...]
    o_ref[0] = (y * (1.0 - lambda_init)).astype(o_ref.dtype)


def _diff_kernel(q_ref, k_ref, v_ref, kc_ref, vc_ref, lam_ref, onorm_ref, o_ref, *, lambda_init):
    lam = _diff_lambda(lam_ref, lambda_init)
    q = q_ref[0]
    k, v, kc, vc = k_ref[0], v_ref[0], kc_ref[0], vc_ref[0]
    tq = q.shape[0]
    low = lax.broadcasted_iota(jnp.int32, q.shape, 1) < DIFF_HEAD_DIM
    zero = jnp.zeros_like(q)
    qq = jnp.concatenate([jnp.where(low, q, zero), jnp.where(low, zero, q)], axis=0)
    s_loc = _dot_t(qq, k)
    s_ctx = _dot_t(qq, kc)
    m = jnp.maximum(jnp.max(s_loc, axis=-1, keepdims=True), jnp.max(s_ctx, axis=-1, keepdims=True))
    p_loc = jnp.exp(s_loc - m)
    p_ctx = jnp.exp(s_ctx - m)
    z = jnp.sum(p_loc, axis=-1, keepdims=True) + jnp.sum(p_ctx, axis=-1, keepdims=True)
    pv = (_dot(p_loc.astype(BF16), v) + _dot(p_ctx.astype(BF16), vc)) * (1.0 / z)
    _diff_out(pv[:tq] - lam * pv[tq:], onorm_ref, lambda_init, o_ref)


def _diff_ctx_kernel(q_ref, k_ref, v_ref, lam_ref, onorm_ref, o_ref, *, lambda_init):
    lam = _diff_lambda(lam_ref, lambda_init)
    q, k, v = q_ref[0], k_ref[0], v_ref[0]
    tq = q.shape[0]
    low = lax.broadcasted_iota(jnp.int32, q.shape, 1) < DIFF_HEAD_DIM
    zero = jnp.zeros_like(q)
    qq = jnp.concatenate([jnp.where(low, q, zero), jnp.where(low, zero, q)], axis=0)
    s = _dot_t(qq, k)
    p = jnp.exp(s - jnp.max(s, axis=-1, keepdims=True))
    pv = _dot(p.astype(BF16), v) * (1.0 / jnp.sum(p, axis=-1, keepdims=True))
    _diff_out(pv[:tq] - lam * pv[tq:], onorm_ref, lambda_init, o_ref)


def _diff_attention(q, k, qkv, qc, kc, qkv_c, lam_vecs, out_norm, lambda_init, batch, tq=256):
    d = D_MODEL
    hw = 2 * DIFF_HEAD_DIM
    s = q.shape[0] // batch
    n_ctx = qc.shape[0] // batch
    v_blk = 2 * d // hw
    q3, k3, qkv3 = q.reshape(batch, s, d), k.reshape(batch, s, d), qkv.reshape(batch, s, 3 * d)
    qc3, kc3, qkvc3 = qc.reshape(batch, n_ctx, d), kc.reshape(batch, n_ctx, d), qkv_c.reshape(batch, n_ctx, 3 * d)
    onorm = out_norm.reshape(1, hw)
    ol = pl.pallas_call(
        functools.partial(_diff_kernel, lambda_init=lambda_init),
        out_shape=jax.ShapeDtypeStruct((batch, s, d), BF16), grid=(batch, DIFF_HEADS, s // tq),
        in_specs=[pl.BlockSpec((1, tq, hw), lambda b, h, i: (b, i, h)),
                  pl.BlockSpec((1, s, hw), lambda b, h, i: (b, 0, h)),
                  pl.BlockSpec((1, s, hw), lambda b, h, i: (b, 0, v_blk + h)),
                  pl.BlockSpec((1, n_ctx, hw), lambda b, h, i: (b, 0, h)),
                  pl.BlockSpec((1, n_ctx, hw), lambda b, h, i: (b, 0, v_blk + h)),
                  pl.BlockSpec((4, DIFF_HEAD_DIM), lambda b, h, i: (0, 0)),
                  pl.BlockSpec((1, hw), lambda b, h, i: (0, 0))],
        out_specs=pl.BlockSpec((1, tq, hw), lambda b, h, i: (b, i, h)),
        compiler_params=_params("parallel", "arbitrary", "arbitrary"), name="diff_attention",
    )(q3, k3, qkv3, kc3, qkvc3, lam_vecs, onorm)
    oc = pl.pallas_call(
        functools.partial(_diff_ctx_kernel, lambda_init=lambda_init),
        out_shape=jax.ShapeDtypeStruct((batch, n_ctx, d), BF16), grid=(batch, DIFF_HEADS),
        in_specs=[pl.BlockSpec((1, n_ctx, hw), lambda b, h: (b, 0, h)),
                  pl.BlockSpec((1, n_ctx, hw), lambda b, h: (b, 0, h)),
                  pl.BlockSpec((1, n_ctx, hw), lambda b, h: (b, 0, v_blk + h)),
                  pl.BlockSpec((4, DIFF_HEAD_DIM), lambda b, h: (0, 0)),
                  pl.BlockSpec((1, hw), lambda b, h: (0, 0))],
        out_specs=pl.BlockSpec((1, n_ctx, hw), lambda b, h: (b, 0, h)),
        compiler_params=_params("parallel", "arbitrary"), name="diff_ctx_attention",
    )(qc3, kc3, qkvc3, lam_vecs, onorm)
    return ol.reshape(batch * s, d), oc.reshape(batch * n_ctx, d)


def _conv_kernel(x_ref, w_ref, o_ref, *, act, n_norm_q, n_norm, qscale):
    x = x_ref[0].astype(F32)
    seq = x.shape[0]
    row = lax.broadcasted_iota(jnp.int32, x.shape, 0)
    prev = jnp.where(row == 0, 0.0, pltpu.roll(x, 1, 0))
    nxt = jnp.where(row == seq - 1, 0.0, pltpu.roll(x, seq - 1, 0))
    w = w_ref[...]
    y = prev * w[0:1] + x * w[1:2] + nxt * w[2:3]
    if act:
        y = _silu(y)
    if n_norm:
        j = pl.program_id(1)
        parts = []
        for g in range(y.shape[1] // LANES):
            seg = y[:, g * LANES:(g + 1) * LANES]
            inv = lax.rsqrt(jnp.sum(seg * seg, axis=-1, keepdims=True) + L2_EPS)
            scale = jnp.where(j < n_norm_q, inv * qscale, jnp.where(j < n_norm, inv, 1.0))
            parts.append(seg * scale)
        y = jnp.concatenate(parts, axis=-1) if len(parts) > 1 else parts[0]
    o_ref[0] = y.astype(o_ref.dtype)


def _dwconv3(x, w, batch, act=False, l2norm_cols=0, qscale=1.0, tn=256):
    m, c = x.shape
    seq = m // batch
    n_norm = l2norm_cols // tn
    out = pl.pallas_call(
        functools.partial(_conv_kernel, act=act, n_norm_q=n_norm // 2, n_norm=n_norm, qscale=qscale),
        out_shape=jax.ShapeDtypeStruct((batch, seq, c), BF16), grid=(batch, c // tn),
        in_specs=[pl.BlockSpec((1, seq, tn), lambda b, j: (b, 0, j)),
                  pl.BlockSpec((3, tn), lambda b, j: (0, j))],
        out_specs=pl.BlockSpec((1, seq, tn), lambda b, j: (b, 0, j)),
        compiler_params=_params("parallel", "arbitrary"), name="dwconv3",
    )(x.reshape(batch, seq, c), w)
    return out.reshape(m, c)


def _gdn_gate_kernel(h_ref, w_ref, wt_ref, a_ref, at_ref, bias_ref, biast_ref, col_ref, row_ref):
    h = h_ref[...]
    col = _dot(h, w_ref[...])
    row = _dot_t(wt_ref[...], h)

    def act(z, neg_a, bias, is_g):
        zb = z + bias
        softplus = jnp.maximum(zb, 0.0) + jnp.log(1.0 + jnp.exp(-jnp.abs(zb)))
        return jnp.where(is_g, neg_a * softplus, 1.0 / (1.0 + jnp.exp(-z)))

    lane = lax.broadcasted_iota(jnp.int32, col.shape, 1)
    col_ref[...] = act(col, a_ref[...], bias_ref[...], lane < 2 * GDN_HEADS)
    sub = lax.broadcasted_iota(jnp.int32, row.shape, 0)
    row_ref[...] = act(row, at_ref[...], biast_ref[...], sub < 2 * GDN_HEADS)


def _gdn_gates(h, w_beta, w_decay, a_log, dt_bias, tm=512):
    m, d = h.shape
    nh = GDN_HEADS
    tm = min(tm, m)
    w = jnp.concatenate([w_decay[0], w_decay[1], w_beta[0], w_beta[1]], axis=-1)
    wpad = jnp.zeros((d, LANES), F32).at[:, :4 * nh].set(w).astype(BF16)
    wt = w.T.astype(BF16)
    neg_a = jnp.concatenate([-jnp.exp(a_log[0]), -jnp.exp(a_log[1]), jnp.zeros((2 * nh,), F32)])
    bias = jnp.concatenate([dt_bias[0], dt_bias[1], jnp.zeros((2 * nh,), F32)])
    pad = lambda v: jnp.zeros((1, LANES), F32).at[0, :4 * nh].set(v)
    return pl.pallas_call(
        _gdn_gate_kernel,
        out_shape=(jax.ShapeDtypeStruct((m, LANES), F32), jax.ShapeDtypeStruct((4 * nh, m), F32)),
        grid=(m // tm,),
        in_specs=[pl.BlockSpec((tm, d), lambda i: (i, 0)),
                  pl.BlockSpec((d, LANES), lambda i: (0, 0)),
                  pl.BlockSpec((4 * nh, d), lambda i: (0, 0)),
                  pl.BlockSpec((1, LANES), lambda i: (0, 0)),
                  pl.BlockSpec((4 * nh, 1), lambda i: (0, 0)),
                  pl.BlockSpec((1, LANES), lambda i: (0, 0)),
                  pl.BlockSpec((4 * nh, 1), lambda i: (0, 0))],
        out_specs=(pl.BlockSpec((tm, LANES), lambda i: (i, 0)), pl.BlockSpec((4 * nh, tm), lambda i: (0, i))),
        compiler_params=_params("parallel"), name="gdn_gates",
    )(h, wpad, wt, pad(neg_a), neg_a.reshape(4 * nh, 1), pad(bias), bias.reshape(4 * nh, 1))


def _gdn_local_kernel(q_ref, k_ref, v_ref, col_ref, row_ref,
                      u_ref, w_ref, qg_ref, kd_ref, att_ref, dl_ref):
    c = GDN_CHUNK
    nh = GDN_HEADS
    hd = GDN_HEAD_DIM
    ii = lax.broadcasted_iota(jnp.int32, (c, c), 0)
    jj = lax.broadcasted_iota(jnp.int32, (c, c), 1)
    eye = (ii == jj).astype(F32)
    incl = [ii >= jj, ii <= jj]
    strict = [ii > jj, ii < jj]
    col = col_ref[0]
    row = row_ref[0, 0]
    gc_col, gc_row = [], []
    for d in range(2):
        m_col = incl[d].astype(F32)
        m_row = incl[1 - d].astype(F32)
        gc_col.append(jnp.dot(m_col, col[:, d * nh:(d + 1) * nh], preferred_element_type=F32, precision=HIGHEST))
        gc_row.append(jnp.dot(row[d * nh:(d + 1) * nh], m_row, preferred_element_type=F32, precision=HIGHEST))
    base = GDN_SOLVE_BASE
    same_base = (ii // base) == (jj // base)
    a_mats, rhs, tails = [], [], []
    for h in range(nh):
        cs = slice(h * hd, (h + 1) * hd)
        q, k, v = q_ref[0, :, cs], k_ref[0, :, cs], v_ref[0, :, cs]
        qf, kf, vf = q.astype(F32), k.astype(F32), v.astype(F32)
        kk = _dot_t(k, k)
        qk = _dot_t(q, k)
        for d in range(2):
            gcc = gc_col[d][:, h:h + 1]
            gcr = gc_row[d][h:h + 1, :]
            beta = col[:, 2 * nh + d * nh + h:2 * nh + d * nh + h + 1]
            dec = jnp.exp(jnp.where(incl[d], gcc - gcr, NEG_BIG))
            a_mats.append(jnp.where(strict[d], beta * kk * dec, 0.0))
            eg = jnp.exp(gcc)
            rhs.append(jnp.concatenate([vf * beta, kf * (beta * eg)], axis=-1).astype(BF16))
            g_last = gcr[:, c - 1:c] if d == 0 else gcr[:, 0:1]
            qg_ref[d, 0, :, cs] = (qf * eg).astype(qg_ref.dtype)
            kd_ref[d, 0, :, cs] = (kf * jnp.exp(g_last - gcc)).astype(kd_ref.dtype)
            att_ref[d, 0, 0, h] = (qk * dec).astype(att_ref.dtype)
            dl_ref[d, 0, 0, h:h + 1, :] = jnp.broadcast_to(jnp.exp(g_last), (1, hd))
            tails.append((d, cs))
    bdot = lambda x, y: _dot(x.astype(BF16), y.astype(BF16))
    npow = [jnp.where(same_base, -a, 0.0) for a in a_mats]
    inv = [eye + n for n in npow]
    span = 2
    while span < base:
        npow = [bdot(n, n) for n in npow]
        inv = [p + bdot(p, n) for p, n in zip(inv, npow)]
        span *= 2
    size = base
    while size < c:
        merge = jnp.logical_and((ii // (2 * size)) == (jj // (2 * size)), (ii // size) != (jj // size))
        low = [bdot(jnp.where(merge, a, 0.0), p) for a, p in zip(a_mats, inv)]
        inv = [p - bdot(p, x) for p, x in zip(inv, low)]
        size *= 2
    for p, r, (d, cs) in zip(inv, rhs, tails):
        sol = _dot(p.astype(BF16), r)
        u_ref[d, 0, :, cs] = sol[:, :hd]
        w_ref[d, 0, :, cs] = sol[:, hd:].astype(w_ref.dtype)


def _gdn_scan_kernel(uf_ref, wf_ref, qgf_ref, kdf_ref, attf_ref, dlf_ref,
                     ub_ref, wb_ref, qgb_ref, kdb_ref, attb_ref, dlb_ref, s0_ref,
                     of_ref, ob_ref, s_ref):
    hd = GDN_HEAD_DIM

    @pl.when(pl.program_id(1) == 0)
    def _():
        s_ref[...] = s0_ref[...]

    dirs = ((uf_ref, wf_ref, qgf_ref, kdf_ref, attf_ref, dlf_ref, of_ref),
            (ub_ref, wb_ref, qgb_ref, kdb_ref, attb_ref, dlb_ref, ob_ref))
    pairs = [(d, h) for d in range(2) for h in range(GDN_HEADS)]
    cols = lambda h: slice(h * hd, (h + 1) * hd)
    states = [s_ref[0, d, h] for d, h in pairs]
    sbs = [s.astype(BF16) for s in states]
    v_new = [dirs[d][0][0, 0, :, cols(h)] - _dot(dirs[d][1][0, 0, :, cols(h)], sb)
             for (d, h), sb in zip(pairs, sbs)]
    o_state = [_dot(dirs[d][2][0, 0, :, cols(h)], sb) for (d, h), sb in zip(pairs, sbs)]
    vbs = [v.astype(BF16) for v in v_new]
    for (d, h), o1, vb, s in zip(pairs, o_state, vbs, states):
        dirs[d][6][0, :, cols(h)] = o1 + _dot(dirs[d][4][0, 0, 0, h], vb)
        upd = lax.dot_general(dirs[d][3][0, 0, :, cols(h)], vb, (((0,), (0,)), ((), ())),
                              preferred_element_type=F32)
        s_ref[0, d, h] = s * dirs[d][5][0, 0, 0, h:h + 1, :] + upd


def _gdn_core(u3, gcol, grow, s0, batch):
    c = GDN_CHUNK
    nh, hd = GDN_HEADS, GDN_HEAD_DIM
    d = nh * hd
    m = u3.shape[0]
    seq = m // batch
    n = seq // c
    u33 = u3.reshape(batch, seq, 3 * d)
    gcol3 = gcol.reshape(batch, seq, LANES)
    grow4 = jnp.transpose(grow.reshape(4 * nh, batch, n, c), (1, 2, 0, 3))
    big = lambda dt: jax.ShapeDtypeStruct((2, batch, seq, d), dt)
    blk = pl.BlockSpec((2, 1, c, d), lambda b, i: (0, b, i, 0))
    u, w, qg, kd, att, dl = pl.pallas_call(
        _gdn_local_kernel,
        out_shape=(big(F32), big(BF16), big(BF16), big(BF16),
                   jax.ShapeDtypeStruct((2, batch, n, nh, c, c), BF16),
                   jax.ShapeDtypeStruct((2, batch, n, nh, hd), F32)),
        grid=(batch, n),
        in_specs=[pl.BlockSpec((1, c, d), lambda b, i: (b, i, 0)),
                  pl.BlockSpec((1, c, d), lambda b, i: (b, i, 1)),
                  pl.BlockSpec((1, c, d), lambda b, i: (b, i, 2)),
                  pl.BlockSpec((1, c, LANES), lambda b, i: (b, i, 0)),
                  pl.BlockSpec((1, 1, 4 * nh, c), lambda b, i: (b, i, 0, 0))],
        out_specs=(blk, blk, blk, blk,
                   pl.BlockSpec((2, 1, 1, nh, c, c), lambda b, i: (0, b, i, 0, 0, 0)),
                   pl.BlockSpec((2, 1, 1, nh, hd), lambda b, i: (0, b, i, 0, 0))),
        compiler_params=_params("parallel", "arbitrary"), name="gdn_local",
    )(u33, u33, u33, gcol3, grow4)

    def dir_specs(dd):
        pos = (lambda i: i) if dd == 0 else (lambda i: n - 1 - i)
        big_blk = pl.BlockSpec((1, 1, c, d), lambda b, i: (dd, b, pos(i), 0))
        return [big_blk, big_blk, big_blk, big_blk,
                pl.BlockSpec((1, 1, 1, nh, c, c), lambda b, i: (dd, b, pos(i), 0, 0, 0)),
                pl.BlockSpec((1, 1, 1, nh, hd), lambda b, i: (dd, b, pos(i), 0, 0))]

    s_blk = pl.BlockSpec((1, 2, nh, hd, hd), lambda b, i: (b, 0, 0, 0, 0))
    o_f, o_b, s_fin = pl.pallas_call(
        _gdn_scan_kernel,
        out_shape=(jax.ShapeDtypeStruct((batch, seq, d), F32), jax.ShapeDtypeStruct((batch, seq, d), F32),
                   jax.ShapeDtypeStruct((batch, 2, nh, hd, hd), F32)),
        grid=(batch, n),
        in_specs=dir_specs(0) + dir_specs(1) + [s_blk],
        out_specs=(pl.BlockSpec((1, c, d), lambda b, i: (b, i, 0)),
                   pl.BlockSpec((1, c, d), lambda b, i: (b, n - 1 - i, 0)), s_blk),
        compiler_params=_params("parallel", "arbitrary"), name="gdn_scan",
    )(u, w, qg, kd, att, dl, u, w, qg, kd, att, dl, s0)
    return o_f.reshape(m, d), o_b.reshape(m, d), s_fin


def _gdn_out_kernel(of_ref, ob_ref, gate_ref, norm_ref, o_ref):
    o = of_ref[...] + ob_ref[...]
    gate = _silu(gate_ref[...].astype(F32))
    parts = []
    for h in range(GDN_HEADS):
        seg = o[:, h * GDN_HEAD_DIM:(h + 1) * GDN_HEAD_DIM]
        parts.append(seg * lax.rsqrt(jnp.mean(seg * seg, axis=-1, keepdims=True) + RMS_EPS) * norm_ref[...])
    o_ref[...] = (jnp.concatenate(parts, axis=-1) * gate).astype(o_ref.dtype)


def _gdn_out(o_f, o_b, gate_lin, out_norm, tm=512):
    m, d = o_f.shape
    tm = min(tm, m)
    blk = pl.BlockSpec((tm, d), lambda i: (i, 0))
    return pl.pallas_call(
        _gdn_out_kernel, out_shape=jax.ShapeDtypeStruct((m, d), BF16), grid=(m // tm,),
        in_specs=[blk, blk, blk, pl.BlockSpec((1, GDN_HEAD_DIM), lambda i: (0, 0))],
        out_specs=blk, compiler_params=_params("parallel"), name="gdn_out",
    )(o_f, o_b, gate_lin, out_norm.reshape(1, GDN_HEAD_DIM))


def _gdn_branch(h, batch, s0, p):
    d = GDN_HEADS * GDN_HEAD_DIM
    lin = _matmul(h, p['w_qkv'])
    u3 = _dwconv3(lin, p['conv'], batch, act=True, l2norm_cols=2 * d, qscale=GDN_HEAD_DIM ** -0.5)
    gcol, grow = _gdn_gates(h, p['w_beta'], p['w_decay'], p['a_log'], p['dt_bias'])
    o_f, o_b, s_fin = _gdn_core(u3, gcol, grow, s0, batch)
    gate_lin = _matmul(h, p['w_gate'])
    return _gdn_out(o_f, o_b, gate_lin, p['out_norm']), s_fin


def _hy_filter_kernel(z_ref, w1_ref, b1_ref, fr_ref, w2_ref, b2_ref, w3f_ref, b3f_ref, w3b_ref, b3b_ref,
                      dl_ref, hf_ref, hb_ref):
    hdot = functools.partial(jnp.dot, preferred_element_type=F32, precision=HIGHEST)
    z = z_ref[...]
    fr = fr_ref[...]
    h1 = jnp.sin(fr[0:1] * (hdot(z, w1_ref[...]) + b1_ref[...]))
    h2 = jnp.sin(fr[1:2] * (hdot(h1, w2_ref[...]) + b2_ref[...]))
    decay = jnp.exp(-z[:, 0:1] * dl_ref[...])
    hf = (hdot(h2, w3f_ref[...]) + b3f_ref[...]) * decay
    hb = (hdot(h2, w3b_ref[...]) + b3b_ref[...]) * decay
    inv = 1.0 / (jnp.sum(jnp.abs(hf), axis=0, keepdims=True) + jnp.sum(jnp.abs(hb), axis=0, keepdims=True))
    hf_ref[0] = hf * inv
    hb_ref[0] = hb * inv


def _hyena_filters(seq, p, tn=256):
    d = D_MODEL
    width = p['w2'].shape[0]
    t = np.linspace(0.0, 1.0, seq)[:, None]
    bands = (HY_EMB_DIM - 1) // 2
    ang = (2.0 * math.pi * np.arange(seq) / seq)[:, None] * np.linspace(1e-4, bands - 1, bands)[None, :]
    feats = np.zeros((seq, LANES), np.float32)
    feats[:, :HY_EMB_DIM] = np.concatenate([t, np.cos(ang), -np.sin(ang)], axis=-1)
    w1 = jnp.zeros((LANES, width), F32).at[:HY_EMB_DIM].set(p['w1'])
    deltas = np.abs(np.linspace(HY_MIN_DECAY, HY_MAX_DECAY, d)).astype(np.float32)[None, :]
    nj = d // tn
    full = lambda shape: pl.BlockSpec(shape, lambda n, j: (0,) * len(shape))
    w3 = lambda dd: pl.BlockSpec((width, tn), lambda n, j: (0, (2 * n + dd) * nj + j))
    b3 = lambda dd: pl.BlockSpec((1, tn), lambda n, j: (0, (2 * n + dd) * nj + j))
    out = jax.ShapeDtypeStruct((HY_ORDER, seq, d), F32)
    oblk = pl.BlockSpec((1, seq, tn), lambda n, j: (n, 0, j))
    b3row = p['b3'].reshape(1, -1)
    return pl.pallas_call(
        _hy_filter_kernel, out_shape=(out, out), grid=(HY_ORDER, nj),
        in_specs=[full((seq, LANES)), full((LANES, width)), full((1, width)), full((2, width)),
                  full((width, width)), full((1, width)), w3(0), b3(0), w3(1), b3(1),
                  pl.BlockSpec((1, tn), lambda n, j: (0, j))],
        out_specs=(oblk, oblk), compiler_params=_params("arbitrary", "arbitrary"), name="hyena_filters",
    )(jnp.asarray(feats), w1, p['b1'].reshape(1, width), p['freq'], p['w2'], p['b2'].reshape(1, width),
      p['w3'], b3row, p['w3'], b3row, jnp.asarray(deltas))


def _dft_matrices(seq):
    f = lax.broadcasted_iota(jnp.int32, (seq, seq), 0)
    t = lax.broadcasted_iota(jnp.int32, (seq, seq), 1)
    ang = ((f * t) % (2 * seq)).astype(F32) * (math.pi / seq)
    nyq = (1 - 2 * (t % 2)).astype(F32)
    fwd = jnp.stack([jnp.cos(ang), jnp.where(f == 0, nyq, -jnp.sin(ang))])
    wgt = jnp.where(lax.broadcasted_iota(jnp.int32, (1, 1, seq), 2) == 0, 0.5 / seq, 1.0 / seq)
    inv = jnp.transpose(fwd, (0, 2, 1)) * wgt
    return fwd.astype(BF16), inv.astype(BF16)


def _hy_spectrum_kernel(f_ref, hf_ref, hb_ref, hr_ref, hi_ref, t_ref):
    hf = hf_ref[0]
    row = lax.broadcasted_iota(jnp.int32, hf.shape, 0)
    hb = jnp.where(row == 0, 0.0, hb_ref[0])
    hs = (hf + hb).astype(BF16)
    hd = (hf - hb).astype(BF16)
    hr = _dot(f_ref[0], hs)
    hi = _dot(f_ref[1], hd)
    nyq = _dot(f_ref[1, 0:8, :], hs)[0:1]
    orow = lax.broadcasted_iota(jnp.int32, hr.shape, 0)
    first = jnp.logical_and(pl.program_id(0) == 0, orow == 0)
    hr_ref[0] = hr
    hi_ref[0] = jnp.where(first, 0.0, hi)
    t_ref[0] = jnp.where(first, nyq, hr)


def _hyena_spectrum(hf, hb, fwd, fm=1024, tn=256):
    n_ord, seq, d = hf.shape
    fm = min(fm, seq)
    out = jax.ShapeDtypeStruct((n_ord, seq, d), F32)
    hblk = pl.BlockSpec((1, seq, tn), lambda c, n, j: (n, 0, j))
    oblk = pl.BlockSpec((1, fm, tn), lambda c, n, j: (n, c, j))
    return pl.pallas_call(
        _hy_spectrum_kernel, out_shape=(out, out, out), grid=(seq // fm, n_ord, d // tn),
        in_specs=[pl.BlockSpec((2, fm, seq), lambda c, n, j: (0, c, 0)), hblk, hblk],
        out_specs=(oblk, oblk, oblk),
        compiler_params=_params("arbitrary", "arbitrary", "arbitrary"), name="hyena_spectrum",
    )(fwd, hf, hb)


def _hy_fwd_kernel(f_ref, z_ref, hr_ref, hi_ref, t_ref, y_ref):
    u = z_ref[0]
    xr = _dot(f_ref[0], u)
    xi = _dot(f_ref[1], u)
    hr, hi, tt = hr_ref[0], hi_ref[0], t_ref[0]
    y_ref[0, 0] = (xr * hr - xi * hi).astype(y_ref.dtype)
    y_ref[0, 1] = (xr * hi + xi * tt).astype(y_ref.dtype)


def _hyena_fwd(z3, z_col0, fwd, hr, hi, tt, order, fm=1024, tn=256):
    batch, seq, _ = z3.shape
    d = D_MODEL
    fm = min(fm, seq)
    zoff = z_col0 // tn
    hblk = pl.BlockSpec((1, fm, tn), lambda c, b, j: (order, c, j))
    return pl.pallas_call(
        _hy_fwd_kernel, out_shape=jax.ShapeDtypeStruct((batch, 2, seq, d), BF16),
        grid=(seq // fm, batch, d // tn),
        in_specs=[pl.BlockSpec((2, fm, seq), lambda c, b, j: (0, c, 0)),
                  pl.BlockSpec((1, seq, tn), lambda c, b, j: (b, 0, zoff + j)), hblk, hblk, hblk],
        out_specs=pl.BlockSpec((1, 2, fm, tn), lambda c, b, j: (b, 0, c, j)),
        compiler_params=_params("arbitrary", "arbitrary", "arbitrary"), name="hyena_fwd_dft",
    )(fwd, z3, hr, hi, tt)


def _hy_inv_kernel(g_ref, y_ref, z_ref, gate_ref, skip_ref, o_ref):
    y = _dot(g_ref[0], y_ref[0, 0]) + _dot(g_ref[1], y_ref[0, 1])
    conv = y + z_ref[0].astype(F32) * skip_ref[...]
    o_ref[0] = (gate_ref[0].astype(F32) * conv).astype(o_ref.dtype)


def _hyena_inv(y, inv, z3, z_col0, gate3, gate_col0, skip, tmc=1024, tn=256):
    batch, _, seq, d = y.shape
    tmc = min(tmc, seq)
    zoff, goff = z_col0 // tn, gate_col0 // tn
    return pl.pallas_call(
        _hy_inv_kernel, out_shape=jax.ShapeDtypeStruct((batch, seq, d), BF16),
        grid=(seq // tmc, batch, d // tn),
        in_specs=[pl.BlockSpec((2, tmc, seq), lambda c, b, j: (0, c, 0)),
                  pl.BlockSpec((1, 2, seq, tn), lambda c, b, j: (b, 0, 0, j)),
                  pl.BlockSpec((1, tmc, tn), lambda c, b, j: (b, c, zoff + j)),
                  pl.BlockSpec((1, tmc, tn), lambda c, b, j: (b, c, goff + j)),
                  pl.BlockSpec((1, tn), lambda c, b, j: (0, j))],
        out_specs=pl.BlockSpec((1, tmc, tn), lambda c, b, j: (b, c, j)),
        compiler_params=_params("arbitrary", "arbitrary", "arbitrary"), name="hyena_inv_dft",
    )(inv, y, z3, gate3, skip.reshape(1, d))


def _hyena_branch(h, batch, p):
    d = D_MODEL
    m = h.shape[0]
    seq = m // batch
    xs = _dwconv3(_matmul(h, p['w_in']), p['conv'], batch).reshape(batch, seq, 3 * d)
    hf, hb = _hyena_filters(seq, p)
    fwd, inv = _dft_matrices(seq)
    hr, hi, tt = _hyena_spectrum(hf, hb, fwd)
    z, z_col0 = xs, 2 * d
    for n in range(HY_ORDER):
        y = _hyena_fwd(z, z_col0, fwd, hr, hi, tt, n)
        z, z_col0 = _hyena_inv(y, inv, z, z_col0, xs, n * d, p['skip'][n]), 0
    return z.reshape(m, d)


DENSE_FFN_PARTS = 2


def kernel(x, c, ctx, c_ctx, ada_w, ada_b, norm_g, na_w_qkv, na_q_norm, na_k_norm, na_rpb, na_w_o, gdn_w_qkv, gdn_conv, gdn_w_gate, gdn_w_beta, gdn_w_decay, gdn_a_log, gdn_dt_bias, gdn_out_norm, gdn_w_o, diff_w_qkv, diff_q_norm, diff_k_norm, diff_lambda, diff_out_norm, diff_w_o, hy_w_in, hy_conv, hy_filt_w1, hy_filt_b1, hy_filt_freq, hy_filt_w2, hy_filt_b2, hy_filt_w3, hy_filt_b3, hy_skip, hy_w_o, ffn_w_in, ffn_w_out, moe_router, moe_w_in, moe_w_out):
    batch, seq, d = x.shape
    n_ctx = ctx.shape[1]
    depth = ada_w.shape[0]
    bf = lambda w: w.astype(BF16)

    n_rows = -(-(batch + 1) // 8) * 8
    cc = jnp.zeros((n_rows, d), F32).at[:batch].set(c).at[batch].set(c_ctx)
    mods = _ada_mods(cc, ada_w, ada_b)

    xl = x.reshape(batch * seq, d)
    xc = ctx.reshape(batch * n_ctx, d)
    for i in range(depth):
        last = i == depth - 1
        kind = i % 4
        ml = [mods[i, :batch, k * d:(k + 1) * d].reshape(batch, 1, d) for k in range(6)]
        mc = [mods[i, batch:batch + 1, k * d:(k + 1) * d].reshape(1, 1, d) for k in range(6)]
        ctx_needed = (not last) or kind != 3
        hl = _norm_modulate(xl, norm_g[i, 0], ml[0], ml[1])
        hc = _norm_modulate(xc, norm_g[i, 0], mc[0], mc[1]) if ctx_needed else None
        oc = None
        if kind == 0:
            w = bf(na_w_qkv)
            qkv, qkv_c = _matmul(hl, w), _matmul(hc, w)
            gq = jnp.tile(na_q_norm, NA_HEADS) * NA_HEAD_DIM ** -0.5
            gk = jnp.tile(na_k_norm, NA_HEADS)
            ol, oc = _neighbourhood_attention(
                _head_norm(qkv, 0, gq, NA_HEAD_DIM), _head_norm(qkv, 1, gk, NA_HEAD_DIM), qkv,
                _head_norm(qkv_c, 0, gq, NA_HEAD_DIM), _head_norm(qkv_c, 1, gk, NA_HEAD_DIM), qkv_c,
                na_rpb, batch)
            w_o = bf(na_w_o)
        elif kind == 1:
            p = dict(w_qkv=bf(gdn_w_qkv), conv=gdn_conv, w_gate=bf(gdn_w_gate), w_beta=gdn_w_beta,
                     w_decay=gdn_w_decay, a_log=gdn_a_log, dt_bias=gdn_dt_bias, out_norm=gdn_out_norm)
            zeros = jnp.zeros((batch, 2, GDN_HEADS, GDN_HEAD_DIM, GDN_HEAD_DIM), F32)
            oc, s_ctx = _gdn_branch(hc, batch, zeros, p)
            ol, _ = _gdn_branch(hl, batch, s_ctx, p)
            w_o = bf(gdn_w_o)
        elif kind == 2:
            lambda_init = 0.8 - 0.6 * math.exp(-0.3 * i)
            w = bf(diff_w_qkv)
            qkv, qkv_c = _matmul(hl, w), _matmul(hc, w)
            reps = d // DIFF_HEAD_DIM
            gq = jnp.tile(diff_q_norm, reps) * DIFF_HEAD_DIM ** -0.5
            gk = jnp.tile(diff_k_norm, reps)
            ol, oc = _diff_attention(
                _head_norm(qkv, 0, gq, DIFF_HEAD_DIM, seq=seq, rope=True),
                _head_norm(qkv, 1, gk, DIFF_HEAD_DIM, seq=seq, rope=True), qkv,
                _head_norm(qkv_c, 0, gq, DIFF_HEAD_DIM), _head_norm(qkv_c, 1, gk, DIFF_HEAD_DIM), qkv_c,
                diff_lambda, diff_out_norm, lambda_init, batch)
            w_o = bf(diff_w_o)
        else:
            p = dict(w_in=bf(hy_w_in), conv=hy_conv, w1=hy_filt_w1, b1=hy_filt_b1, freq=hy_filt_freq,
                     w2=hy_filt_w2, b2=hy_filt_b2, w3=hy_filt_w3, b3=hy_filt_b3, skip=hy_skip)
            ol = _hyena_branch(hl, batch, p)
            oc = _hyena_branch(hc, batch, p) if ctx_needed else None
            w_o = bf(hy_w_o)
        xl = _matmul_residual(ol, w_o, xl, ml[2])
        if not last:
            xc = _matmul_residual(oc, w_o, xc, mc[2])

        j = i // 2
        streams = [(xl, ml)] if last else [(xl, ml), (xc, mc)]
        outs = []
        for xs, ms in streams:
            if i % 2 == 0:
                hs = _norm_modulate(xs, norm_g[i, 1], ms[3], ms[4])
                w_in = bf(ffn_w_in[j]).reshape(1, d, -1)
                w_out = bf(ffn_w_out[j]).reshape(DENSE_FFN_PARTS, -1, d)
                outs.append(_dense_ffn(hs, w_in, w_out, xs, ms[5]))
            else:
                outs.append(_moe(xs, norm_g[i, 1], ms[3], ms[4], ms[5], moe_router[j],
                                 bf(moe_w_in[j]), bf(moe_w_out[j])))
        xl = outs[0]
        if not last:
            xc = outs[1]
    return xl.reshape(batch, seq, d)
```

```python
import functools
import math

import numpy as np
import jax
import jax.numpy as jnp
from jax import lax
from jax.experimental import pallas as pl
from jax.experimental.pallas import tpu as pltpu

F32 = jnp.float32
BF16 = jnp.bfloat16
HIGHEST = lax.Precision.HIGHEST

VMEM_LIMIT_BYTES = 56 * 1024 * 1024
LANES = 128

D_MODEL = 1024
GRID_W = 64
RMS_EPS = 1e-6
L2_EPS = 1e-6
ROPE_BASE = 10000.0
NA_HEADS = 16
NA_HEAD_DIM = 64
NA_WIN_ROWS = 8
NA_WIN_COLS = 16
GDN_HEADS = 8
GDN_HEAD_DIM = 128
GDN_CHUNK = 64
GDN_SOLVE_BASE = 8
DIFF_HEADS = 8
DIFF_HEAD_DIM = 64
HY_ORDER = 2
HY_EMB_DIM = 33
HY_MAX_DECAY = math.log(1e-2) / 0.3
HY_MIN_DECAY = math.log(1e-2) / 1.5
MOE_EXPERTS = 8
NEG_BIG = -1e30


def _params(*sem):
    return pltpu.CompilerParams(dimension_semantics=sem, vmem_limit_bytes=VMEM_LIMIT_BYTES)


def _dot(a, b):
    return jnp.dot(a, b, preferred_element_type=F32)


def _dot_t(a, b):
    return lax.dot_general(a, b, (((1,), (1,)), ((), ())), preferred_element_type=F32)


def _silu(x):
    return x * (1.0 / (1.0 + jnp.exp(-x)))


def _ada_kernel(c_ref, w_ref, b_ref, o_ref):
    s = _silu(c_ref[...])
    o_ref[0] = jnp.dot(s, w_ref[0], preferred_element_type=F32, precision=HIGHEST) + b_ref[0]


def _ada_mods(cc, ada_w, ada_b):
    depth, d, n = ada_w.shape
    r = cc.shape[0]
    tn = 1024
    return pl.pallas_call(
        _ada_kernel,
        out_shape=jax.ShapeDtypeStruct((depth, r, n), F32),
        grid=(depth, n // tn),
        in_specs=[pl.BlockSpec((r, d), lambda i, j: (0, 0)),
                  pl.BlockSpec((1, d, tn), lambda i, j: (i, 0, j)),
                  pl.BlockSpec((1, 1, tn), lambda i, j: (i, 0, j))],
        out_specs=pl.BlockSpec((1, r, tn), lambda i, j: (i, 0, j)),
        compiler_params=_params("arbitrary", "arbitrary"),
        name="ada_mods",
    )(cc, ada_w, ada_b.reshape(depth, 1, n))


def _norm_mod(x, g_ref, sh_ref, sc_ref):
    y = x * lax.rsqrt(jnp.mean(x * x, axis=-1, keepdims=True) + RMS_EPS) * g_ref[...]
    return y * (1.0 + sc_ref[0]) + sh_ref[0]


def _norm_mod_kernel(x_ref, g_ref, sh_ref, sc_ref, o_ref):
    o_ref[...] = _norm_mod(x_ref[...], g_ref, sh_ref, sc_ref).astype(o_ref.dtype)


def _norm_specs(d, rows, tm, grid_rank):
    if grid_rank == 1:
        mod = pl.BlockSpec((1, 1, d), lambda i, *_: (i * tm // rows, 0, 0))
        return [pl.BlockSpec((1, d), lambda i, *_: (0, 0)), mod, mod]
    mod = pl.BlockSpec((1, 1, d), lambda i, j, *_: (i * tm // rows, 0, 0))
    return [pl.BlockSpec((1, d), lambda i, j, *_: (0, 0)), mod, mod]


def _norm_mod_route_kernel(x_ref, g_ref, sh_ref, sc_ref, r_ref, o_ref, info_ref, cnt_ref):
    h = _norm_mod(x_ref[...], g_ref, sh_ref, sc_ref)
    o_ref[...] = h
    logits = jnp.dot(h, r_ref[...], preferred_element_type=F32, precision=HIGHEST)
    lane = lax.broadcasted_iota(jnp.int32, logits.shape, 1).astype(F32)
    l1 = jnp.where(lane < MOE_EXPERTS, logits, NEG_BIG)
    m1 = jnp.max(l1, axis=-1, keepdims=True)
    i1 = jnp.min(jnp.where(l1 == m1, lane, float(LANES)), axis=-1, keepdims=True)
    l2 = jnp.where(lane == i1, NEG_BIG, l1)
    m2 = jnp.max(l2, axis=-1, keepdims=True)
    i2 = jnp.min(jnp.where(l2 == m2, lane, float(LANES)), axis=-1, keepdims=True)
    e2 = jnp.exp(m2 - m1)
    w1 = 1.0 / (1.0 + e2)
    info_ref[...] = (jnp.where(lane == 0.0, i1, 0.0) + jnp.where(lane == 1.0, i2, 0.0)
                     + jnp.where(lane == 2.0, w1, 0.0) + jnp.where(lane == 3.0, e2 * w1, 0.0))
    hit = jnp.logical_or(lane == i1, lane == i2).astype(F32)
    cnt_ref[0] = jnp.sum(hit, axis=0, keepdims=True)


def _norm_modulate(x, g, shift, scale, router=None, tm=512):
    m, d = x.shape
    grp = shift.shape[0]
    rows = m // grp
    tm = min(tm, rows)
    assert rows % tm == 0
    in_specs = [pl.BlockSpec((tm, d), lambda i: (i, 0)),
                pl.BlockSpec((1, d), lambda i: (0, 0)),
                pl.BlockSpec((1, 1, d), lambda i: (i * tm // rows, 0, 0)),
                pl.BlockSpec((1, 1, d), lambda i: (i * tm // rows, 0, 0))]
    args = [x, g.reshape(1, d), shift, scale]
    if router is None:
        return pl.pallas_call(
            _norm_mod_kernel, out_shape=jax.ShapeDtypeStruct((m, d), BF16), grid=(m // tm,),
            in_specs=in_specs, out_specs=pl.BlockSpec((tm, d), lambda i: (i, 0)),
            compiler_params=_params("parallel"), name="norm_mod")(*args)
    rpad = jnp.zeros((d, LANES), F32).at[:, :router.shape[1]].set(router)
    assert tm == ROUTE_TILE
    return pl.pallas_call(
        _norm_mod_route_kernel,
        out_shape=(jax.ShapeDtypeStruct((m, d), F32), jax.ShapeDtypeStruct((m, LANES), F32),
                   jax.ShapeDtypeStruct((m // tm, 1, LANES), F32)),
        grid=(m // tm,),
        in_specs=in_specs + [pl.BlockSpec((d, LANES), lambda i: (0, 0))],
        out_specs=(pl.BlockSpec((tm, d), lambda i: (i, 0)), pl.BlockSpec((tm, LANES), lambda i: (i, 0)),
                   pl.BlockSpec((1, 1, LANES), lambda i: (i, 0, 0))),
        compiler_params=_params("parallel"), name="norm_mod_route")(*args, rpad)


def _mm_kernel(x_ref, w_ref, o_ref):
    o_ref[...] = _dot(x_ref[...], w_ref[...]).astype(o_ref.dtype)


def _mm_res_kernel(x_ref, w_ref, res_ref, gate_ref, o_ref):
    o_ref[...] = res_ref[...] + gate_ref[0] * _dot(x_ref[...], w_ref[...])


def _mm_res_norm_kernel(x_ref, w_ref, res_ref, gate_ref, g_ref, sh_ref, sc_ref, o_ref, h_ref):
    y = res_ref[...] + gate_ref[0] * _dot(x_ref[...], w_ref[...])
    o_ref[...] = y
    h_ref[...] = _norm_mod(y, g_ref, sh_ref, sc_ref).astype(h_ref.dtype)


def _matmul(x, w, out_dtype=BF16, tm=1024, tn=1024):
    m, k = x.shape
    n = w.shape[1]
    tm, tn = min(tm, m), min(tn, n)
    assert m % tm == 0 and n % tn == 0
    return pl.pallas_call(
        _mm_kernel, out_shape=jax.ShapeDtypeStruct((m, n), out_dtype), grid=(m // tm, n // tn),
        in_specs=[pl.BlockSpec((tm, k), lambda i, j: (i, 0)), pl.BlockSpec((k, tn), lambda i, j: (0, j))],
        out_specs=pl.BlockSpec((tm, tn), lambda i, j: (i, j)),
        compiler_params=_params("parallel", "arbitrary"), name="matmul")(x, w)


def _matmul_residual(x, w, res, gate, norm=None, tm=1024, tn=1024):
    m, k = x.shape
    n = w.shape[1]
    rows = m // gate.shape[0]
    tm, tn = min(tm, rows), min(tn, n)
    assert rows % tm == 0 and n % tn == 0
    if norm is not None:
        assert tn == n
        blk = pl.BlockSpec((tm, n), lambda i, j: (i, 0))
        return pl.pallas_call(
            _mm_res_norm_kernel,
            out_shape=(jax.ShapeDtypeStruct((m, n), F32), jax.ShapeDtypeStruct((m, n), BF16)), grid=(m // tm, 1),
            in_specs=[pl.BlockSpec((tm, k), lambda i, j: (i, 0)), pl.BlockSpec((k, n), lambda i, j: (0, 0)), blk,
                      pl.BlockSpec((1, 1, n), lambda i, j: (i * tm // rows, 0, 0))] + _norm_specs(n, rows, tm, 2),
            out_specs=(blk, blk), compiler_params=_params("parallel", "arbitrary"), name="matmul_residual_norm",
        )(x, w, res, gate, norm[0].reshape(1, n), norm[1], norm[2])
    return pl.pallas_call(
        _mm_res_kernel, out_shape=jax.ShapeDtypeStruct((m, n), F32), grid=(m // tm, n // tn),
        in_specs=[pl.BlockSpec((tm, k), lambda i, j: (i, 0)), pl.BlockSpec((k, tn), lambda i, j: (0, j)),
                  pl.BlockSpec((tm, tn), lambda i, j: (i, j)),
                  pl.BlockSpec((1, 1, tn), lambda i, j: (i * tm // rows, 0, j))],
        out_specs=pl.BlockSpec((tm, tn), lambda i, j: (i, j)),
        compiler_params=_params("parallel", "arbitrary"), name="matmul_residual")(x, w, res, gate)


def _swiglu(x, wa_ref, wb_ref, wo_ref):
    a = _dot(x, wa_ref[0])
    b = _dot(x, wb_ref[0])
    return _dot((_silu(a) * b).astype(BF16), wo_ref[0])


def _ffn_kernel(*refs, with_norm):
    if with_norm:
        h_ref, wa_ref, wb_ref, wo_ref, res_ref, mod_ref, g_ref, sh_ref, sc_ref, o_ref, hn_ref, acc_ref = refs
    else:
        h_ref, wa_ref, wb_ref, wo_ref, res_ref, mod_ref, o_ref, acc_ref = refs
    e = pl.program_id(1)
    y = _swiglu(h_ref[...], wa_ref, wb_ref, wo_ref)

    @pl.when(e == 0)
    def _():
        acc_ref[...] = y

    @pl.when(e > 0)
    def _():
        acc_ref[...] += y

    @pl.when(e == pl.num_programs(1) - 1)
    def _():
        out = res_ref[...] + mod_ref[0] * acc_ref[...]
        o_ref[...] = out
        if with_norm:
            hn_ref[...] = _norm_mod(out, g_ref, sh_ref, sc_ref).astype(hn_ref.dtype)


def _dense_ffn(h, w_in, w_out, res, mod, norm=None, tm=512):
    m, d = h.shape
    rows = m // mod.shape[0]
    tm = min(tm, rows)
    assert rows % tm == 0
    parts, f, _ = w_out.shape
    blk = pl.BlockSpec((tm, d), lambda i, e: (i, 0))
    in_specs = [blk,
                pl.BlockSpec((1, d, f), lambda i, e: (0, 0, e)),
                pl.BlockSpec((1, d, f), lambda i, e: (0, 0, parts + e)),
                pl.BlockSpec((1, f, d), lambda i, e: (e, 0, 0)),
                blk,
                pl.BlockSpec((1, 1, d), lambda i, e: (i * tm // rows, 0, 0))]
    args = [h, w_in, w_in, w_out, res, mod]
    out_shape, out_specs = jax.ShapeDtypeStruct((m, d), F32), blk
    if norm is not None:
        in_specs += _norm_specs(d, rows, tm, 2)
        args += [norm[0].reshape(1, d), norm[1], norm[2]]
        out_shape, out_specs = (out_shape, jax.ShapeDtypeStruct((m, d), BF16)), (blk, blk)
    return pl.pallas_call(
        functools.partial(_ffn_kernel, with_norm=norm is not None),
        out_shape=out_shape, grid=(m // tm, parts), in_specs=in_specs, out_specs=out_specs,
        scratch_shapes=[pltpu.VMEM((tm, d), F32)],
        compiler_params=_params("parallel", "arbitrary"), name="ffn")(*args)


ROUTE_TILE = 512
EXPERT_TILE = 512
DMA_CHUNK = 256


def _route_plan_kernel(cnt_ref, base_ref, te_ref, used_ref):
    hdot = functools.partial(jnp.dot, preferred_element_type=F32, precision=HIGHEST)
    cnt = cnt_ref[...]
    nt = cnt.shape[0]
    tot = jnp.sum(cnt, axis=0, keepdims=True)
    seg = jnp.floor((tot + (EXPERT_TILE - 1)) * (1.0 / EXPERT_TILE)) * EXPERT_TILE
    e_r = lax.broadcasted_iota(jnp.int32, (LANES, LANES), 0)
    e_c = lax.broadcasted_iota(jnp.int32, (LANES, LANES), 1)
    off = hdot(jnp.broadcast_to(seg, (8, LANES)), (e_r < e_c).astype(F32))[0:1]
    t_r = lax.broadcasted_iota(jnp.int32, (nt, nt), 0)
    t_c = lax.broadcasted_iota(jnp.int32, (nt, nt), 1)
    base_ref[...] = hdot((t_c < t_r).astype(F32), cnt) + off
    ends = off + seg
    start = lax.broadcasted_iota(jnp.int32, te_ref.shape, 0).astype(F32) * EXPERT_TILE
    lane = lax.broadcasted_iota(jnp.int32, te_ref.shape, 1)
    done = jnp.where(jnp.logical_and(lane < MOE_EXPERTS, start >= ends), 1.0, 0.0)
    te = jnp.minimum(jnp.sum(done, axis=-1, keepdims=True), MOE_EXPERTS - 1.0)
    te_ref[...] = jnp.broadcast_to(te, te_ref.shape).astype(jnp.int32)
    used = ends[:, MOE_EXPERTS - 1:MOE_EXPERTS] * (1.0 / EXPERT_TILE)
    used_ref[...] = jnp.broadcast_to(used, used_ref.shape).astype(jnp.int32)


def _route_pos_kernel(info_ref, base_ref, pos_ref):
    info = info_ref[...]
    lane = lax.broadcasted_iota(jnp.int32, info.shape, 1).astype(F32)
    hit1 = lane == info[:, 0:1]
    hit2 = lane == info[:, 1:2]
    tm = info.shape[0]
    r_i = lax.broadcasted_iota(jnp.int32, (tm, tm), 0)
    r_j = lax.broadcasted_iota(jnp.int32, (tm, tm), 1)
    earlier = jnp.where(r_j < r_i, 1.0, 0.0).astype(BF16)
    rank = _dot(earlier, jnp.where(jnp.logical_or(hit1, hit2), 1.0, 0.0).astype(BF16))
    p = rank + base_ref[0]
    pos1 = jnp.sum(jnp.where(hit1, p, 0.0), axis=-1, keepdims=True)
    pos2 = jnp.sum(jnp.where(hit2, p, 0.0), axis=-1, keepdims=True)
    pos_ref[...] = jnp.where(lane == 0.0, pos1, jnp.where(lane == 1.0, pos2, 0.0)).astype(jnp.int32)


ROW_COPY_UNROLL = 8


def _row_copies(n_rows, copies_of_row, whole_buffer_copies):
    def start(g, carry):
        for u in range(ROW_COPY_UNROLL):
            for cp in copies_of_row(g * ROW_COPY_UNROLL + u):
                cp.start()
        return carry

    lax.fori_loop(0, n_rows // ROW_COPY_UNROLL, start, 0)
    for cp in whole_buffer_copies:
        cp.wait()


def _row_token_kernel(pos_ref, tok_ref, *, n_tokens):
    def clear(r, carry):
        tok_ref[r] = 0
        return carry

    def put(t, carry):
        tok_ref[pos_ref[t]] = t
        tok_ref[pos_ref[n_tokens + t]] = t
        return carry

    lax.fori_loop(0, tok_ref.shape[0], clear, 0, unroll=8)
    lax.fori_loop(0, n_tokens, put, 0, unroll=8)


def _gather_kernel(tok_ref, used_ref, h_ref, xs_ref, buf, sem):
    first = pl.program_id(0) * EXPERT_TILE
    live = pl.program_id(0) < used_ref[0]

    @pl.when(live)
    def _():
        _row_copies(EXPERT_TILE,
                    lambda j: [pltpu.make_async_copy(h_ref.at[pl.ds(tok_ref[first + j], 1)],
                                                     buf.at[pl.ds(j, 1)], sem)],
                    [pltpu.make_async_copy(h_ref.at[pl.ds(0, EXPERT_TILE)], buf, sem)])
        xs_ref[...] = buf[...].astype(xs_ref.dtype)

    @pl.when(jnp.logical_not(live))
    def _():
        xs_ref[...] = jnp.zeros_like(xs_ref)


def _ffn_routed_kernel(te_ref, used_ref, x_ref, wa_ref, wb_ref, wo_ref, y_ref):
    live = pl.program_id(0) < used_ref[0]

    @pl.when(live)
    def _():
        y_ref[...] = _swiglu(x_ref[...], wa_ref, wb_ref, wo_ref)

    @pl.when(jnp.logical_not(live))
    def _():
        y_ref[...] = jnp.zeros_like(y_ref)


def _combine_kernel(*refs, n_tokens, with_norm):
    if with_norm:
        pos_ref, ys_ref, res_ref, info_ref, mod_ref, g_ref, sh_ref, sc_ref, o_ref, hn_ref, buf, sem = refs
    else:
        pos_ref, ys_ref, res_ref, info_ref, mod_ref, o_ref, buf, sem = refs
    first = pl.program_id(0) * DMA_CHUNK
    _row_copies(DMA_CHUNK,
                lambda j: [pltpu.make_async_copy(ys_ref.at[pl.ds(pos_ref[k * n_tokens + first + j], 1)],
                                                 buf.at[k, pl.ds(j, 1)], sem) for k in range(2)],
                [pltpu.make_async_copy(ys_ref.at[pl.ds(0, DMA_CHUNK)], buf.at[k], sem) for k in range(2)])
    info = info_ref[...]
    y = info[:, 2:3] * buf[0] + info[:, 3:4] * buf[1]
    out = res_ref[...] + mod_ref[0] * y
    o_ref[...] = out
    if with_norm:
        hn_ref[...] = _norm_mod(out, g_ref, sh_ref, sc_ref).astype(hn_ref.dtype)


def _moe(x, g, shift, scale, mod, router, w_in, w_out, norm=None):
    m, d = x.shape
    n_e, _, two_f = w_in.shape
    f = two_f // 2
    rows = m // mod.shape[0]
    nt = m // ROUTE_TILE
    n_tiles = 2 * m // EXPERT_TILE + n_e
    n_sorted = n_tiles * EXPERT_TILE
    h32, info, cnt = _norm_modulate(x, g, shift, scale, router=router, tm=ROUTE_TILE)

    te_rows = -(-n_tiles // 8) * 8
    base, te, used = pl.pallas_call(
        _route_plan_kernel,
        out_shape=(jax.ShapeDtypeStruct((nt, LANES), F32), jax.ShapeDtypeStruct((te_rows, LANES), jnp.int32),
                   jax.ShapeDtypeStruct((8, LANES), jnp.int32)),
        name="route_plan")(cnt.reshape(nt, LANES))
    pos = pl.pallas_call(
        _route_pos_kernel, out_shape=jax.ShapeDtypeStruct((m, LANES), jnp.int32), grid=(nt,),
        in_specs=[pl.BlockSpec((ROUTE_TILE, LANES), lambda i: (i, 0)),
                  pl.BlockSpec((1, 1, LANES), lambda i: (i, 0, 0))],
        out_specs=pl.BlockSpec((ROUTE_TILE, LANES), lambda i: (i, 0)),
        compiler_params=_params("parallel"), name="route_pos")(info, base.reshape(nt, 1, LANES))
    pos_flat = jnp.concatenate([pos[:, 0], pos[:, 1]])
    te_flat, used_flat = te[:, 0], used[0, :1]

    any_spec = pl.BlockSpec(memory_space=pl.ANY)
    smem_spec = pl.BlockSpec(memory_space=pltpu.SMEM)
    row_token = pl.pallas_call(
        functools.partial(_row_token_kernel, n_tokens=m),
        out_shape=jax.ShapeDtypeStruct((n_sorted,), jnp.int32),
        in_specs=[smem_spec], out_specs=smem_spec, name="moe_row_token")(pos_flat)
    xs = pl.pallas_call(
        _gather_kernel,
        grid_spec=pltpu.PrefetchScalarGridSpec(
            num_scalar_prefetch=2, grid=(n_tiles,), in_specs=[any_spec],
            out_specs=pl.BlockSpec((EXPERT_TILE, d), lambda i, tok, used: (i, 0)),
            scratch_shapes=[pltpu.VMEM((EXPERT_TILE, d), F32), pltpu.SemaphoreType.DMA(())]),
        out_shape=jax.ShapeDtypeStruct((n_sorted, d), BF16),
        compiler_params=_params("arbitrary"), name="moe_gather",
    )(row_token, used_flat, h32)

    ys = pl.pallas_call(
        _ffn_routed_kernel,
        grid_spec=pltpu.PrefetchScalarGridSpec(
            num_scalar_prefetch=2, grid=(n_tiles,),
            in_specs=[pl.BlockSpec((EXPERT_TILE, d), lambda i, te, used: (i, 0)),
                      pl.BlockSpec((1, d, f), lambda i, te, used: (te[i], 0, 0)),
                      pl.BlockSpec((1, d, f), lambda i, te, used: (te[i], 0, 1)),
                      pl.BlockSpec((1, f, d), lambda i, te, used: (te[i], 0, 0))],
            out_specs=pl.BlockSpec((EXPERT_TILE, d), lambda i, te, used: (i, 0))),
        out_shape=jax.ShapeDtypeStruct((n_sorted, d), F32),
        compiler_params=_params("arbitrary"), name="moe_expert_ffn",
    )(te_flat, used_flat, xs, w_in, w_in, w_out)

    blk = pl.BlockSpec((DMA_CHUNK, d), lambda i, pos: (i, 0))
    in_specs = [any_spec, blk, pl.BlockSpec((DMA_CHUNK, LANES), lambda i, pos: (i, 0)),
                pl.BlockSpec((1, 1, d), lambda i, pos: (i * DMA_CHUNK // rows, 0, 0))]
    args = [pos_flat, ys, x, info, mod]
    out_shape, out_specs = jax.ShapeDtypeStruct((m, d), F32), blk
    if norm is not None:
        in_specs += _norm_specs(d, rows, DMA_CHUNK, 1)
        args += [norm[0].reshape(1, d), norm[1], norm[2]]
        out_shape, out_specs = (out_shape, jax.ShapeDtypeStruct((m, d), BF16)), (blk, blk)
    return pl.pallas_call(
        functools.partial(_combine_kernel, n_tokens=m, with_norm=norm is not None),
        grid_spec=pltpu.PrefetchScalarGridSpec(
            num_scalar_prefetch=1, grid=(m // DMA_CHUNK,), in_specs=in_specs, out_specs=out_specs,
            scratch_shapes=[pltpu.VMEM((2, DMA_CHUNK, d), F32), pltpu.SemaphoreType.DMA(())]),
        out_shape=out_shape, compiler_params=_params("arbitrary"), name="moe_combine",
    )(*args)


def _head_norm_kernel(*refs, group, rope):
    if rope:
        x_ref, gain_ref, ind_ref, ind_t_ref, cos_ref, sa_ref, sb_ref, o_ref = refs
    else:
        x_ref, gain_ref, ind_ref, ind_t_ref, o_ref = refs
    x = x_ref[...].astype(F32)
    ss = _dot((x * x).astype(BF16), ind_ref[...])
    inv = lax.rsqrt(ss * (1.0 / group) + RMS_EPS)
    inv_hi = inv.astype(BF16)
    inv_lo = (inv - inv_hi.astype(F32)).astype(BF16)
    y = x * (_dot(inv_hi, ind_t_ref[...]) + _dot(inv_lo, ind_t_ref[...])) * gain_ref[...]
    if rope:
        d = y.shape[-1]
        rep = d // LANES
        half = group // 4
        wide = lambda t_ref: jnp.concatenate([t_ref[...]] * rep, axis=1)
        y = (y * wide(cos_ref) + pltpu.roll(y, d - half, 1) * wide(sa_ref)
             + pltpu.roll(y, half, 1) * wide(sb_ref))
    o_ref[...] = y.astype(o_ref.dtype)


def _rope_tables(seq):
    pos = np.arange(seq)
    lane = np.arange(LANES) % DIFF_HEAD_DIM
    half = DIFF_HEAD_DIM // 2
    j = lane % half
    inv = ROPE_BASE ** (-(2.0 * (j % (half // 2))) / half)
    p = np.where(lane[None, :] < half, (pos // GRID_W)[:, None], (pos % GRID_W)[:, None])
    ang = p * inv[None, :]
    first = (j < half // 2)[None, :]
    cos, sin = np.cos(ang), np.sin(ang)
    return (jnp.asarray(cos, F32), jnp.asarray(np.where(first, -sin, 0.0), F32),
            jnp.asarray(np.where(first, 0.0, sin), F32))


def _head_norm(src, col_block, gain, group, seq=None, rope=False, tm=512):
    m = src.shape[0]
    d = D_MODEL
    tm = min(tm, m if seq is None else seq)
    assert m % tm == 0
    in_specs = [pl.BlockSpec((tm, d), lambda i: (i, col_block)),
                pl.BlockSpec((1, d), lambda i: (0, 0)),
                pl.BlockSpec((d, LANES), lambda i: (0, 0)),
                pl.BlockSpec((LANES, d), lambda i: (0, 0))]
    ind = (np.arange(d)[:, None] // group == np.arange(LANES)[None, :]).astype(np.float32)
    args = [src, gain.reshape(1, d), jnp.asarray(ind).astype(BF16), jnp.asarray(ind.T).astype(BF16)]
    if rope:
        nblk = seq // tm
        in_specs += [pl.BlockSpec((tm, LANES), lambda i: (i % nblk, 0))] * 3
        args += list(_rope_tables(seq))
    return pl.pallas_call(
        functools.partial(_head_norm_kernel, group=group, rope=rope),
        out_shape=jax.ShapeDtypeStruct((m, d), BF16), grid=(m // tm,),
        in_specs=in_specs, out_specs=pl.BlockSpec((tm, d), lambda i: (i, 0)),
        compiler_params=_params("parallel"), name="head_norm")(*args)


def _na_bias_table(rpb, rows):
    wr = min(NA_WIN_ROWS, rows)
    cols = np.arange(GRID_W)
    c0 = np.clip(cols - NA_WIN_COLS // 2, 0, GRID_W - NA_WIN_COLS)
    col_valid = (cols[None, :] >= c0[:, None]) & (cols[None, :] < c0[:, None] + NA_WIN_COLS)
    col_idx = np.clip(cols[None, :] - cols[:, None], 1 - NA_WIN_COLS, NA_WIN_COLS - 1) + NA_WIN_COLS - 1
    pick = jnp.asarray((col_idx[None] == np.arange(2 * NA_WIN_COLS - 1)[:, None, None]).astype(np.float32))
    lo = NA_WIN_ROWS - 1
    rows_of = jnp.stack([rpb[:, lo - off:lo - off + wr] for off in range(wr)])
    bias = jnp.einsum('ohic,cqk->ohqik', rows_of, pick, precision=HIGHEST)
    bias = jnp.where(col_valid[None, None, :, None, :], bias, NEG_BIG)
    return bias.reshape(wr, NA_HEADS, GRID_W, wr * GRID_W).astype(F32)


def _na_row_start(r, rows, wr):
    return jnp.clip(r - wr // 2, 0, rows - wr)


def _na_kernel(q_ref, k_ref, v_ref, kc_ref, vc_ref, bias_ref, o_ref, *, rows, wr):
    r = pl.program_id(1)
    start = pl.multiple_of(_na_row_start(r, rows, wr) * GRID_W, GRID_W)
    lane = lax.broadcasted_iota(jnp.int32, (GRID_W, LANES), 1)
    low = lane < NA_HEAD_DIM
    win = pl.ds(start, wr * GRID_W)
    cols = lambda h: slice((h // 2) * LANES, (h // 2 + 1) * LANES)
    heads = range(NA_HEADS)
    scores = []
    for h in heads:
        q = q_ref[0, :, cols(h)]
        qm = jnp.where(low if h % 2 == 0 else ~low, q, jnp.zeros_like(q))
        scores.append((_dot_t(qm, k_ref[0, win, cols(h)]) + bias_ref[0, h], _dot_t(qm, kc_ref[0, :, cols(h)])))
    probs = []
    for s_loc, s_ctx in scores:
        m = jnp.maximum(jnp.max(s_loc, axis=-1, keepdims=True), jnp.max(s_ctx, axis=-1, keepdims=True))
        p_loc = jnp.exp(s_loc - m)
        p_ctx = jnp.exp(s_ctx - m)
        z = jnp.sum(p_loc, axis=-1, keepdims=True) + jnp.sum(p_ctx, axis=-1, keepdims=True)
        probs.append((p_loc.astype(BF16), p_ctx.astype(BF16), 1.0 / z))
    outs = [(_dot(p_loc, v_ref[0, win, cols(h)]) + _dot(p_ctx, vc_ref[0, :, cols(h)])) * rz
            for h, (p_loc, p_ctx, rz) in zip(heads, probs)]
    for pair in range(NA_HEADS // 2):
        o_ref[0, :, cols(2 * pair)] = jnp.where(low, outs[2 * pair], outs[2 * pair + 1]).astype(o_ref.dtype)


def _na_ctx_kernel(q_ref, k_ref, v_ref, o_ref):
    lane = lax.broadcasted_iota(jnp.int32, (q_ref.shape[1], LANES), 1)
    low = lane < NA_HEAD_DIM
    for pair in range(NA_HEADS // 2):
        cs = slice(pair * LANES, (pair + 1) * LANES)
        q, k, v = q_ref[0, :, cs], k_ref[0, :, cs], v_ref[0, :, cs]
        outs = []
        for sub in range(2):
            qm = jnp.where(low if sub == 0 else ~low, q, jnp.zeros_like(q))
            s = _dot_t(qm, k)
            p = jnp.exp(s - jnp.max(s, axis=-1, keepdims=True))
            outs.append(_dot(p.astype(BF16), v) * (1.0 / jnp.sum(p, axis=-1, keepdims=True)))
        o_ref[0, :, cs] = jnp.where(low, outs[0], outs[1]).astype(o_ref.dtype)


def _neighbourhood_attention(q, k, qkv, qc, kc, qkv_c, rpb, batch):
    d = D_MODEL
    s = q.shape[0] // batch
    n_ctx = qc.shape[0] // batch
    rows = s // GRID_W
    wr = min(NA_WIN_ROWS, rows)
    bias = _na_bias_table(rpb, rows)
    q3, k3, qkv3 = q.reshape(batch, s, d), k.reshape(batch, s, d), qkv.reshape(batch, s, 3 * d)
    qc3, kc3, qkvc3 = qc.reshape(batch, n_ctx, d), kc.reshape(batch, n_ctx, d), qkv_c.reshape(batch, n_ctx, 3 * d)

    def variant(b, r):
        return (r - _na_row_start(r, rows, wr), 0, 0, 0)

    ol = pl.pallas_call(
        functools.partial(_na_kernel, rows=rows, wr=wr),
        out_shape=jax.ShapeDtypeStruct((batch, s, d), BF16), grid=(batch, rows),
        in_specs=[pl.BlockSpec((1, GRID_W, d), lambda b, r: (b, r, 0)),
                  pl.BlockSpec((1, s, d), lambda b, r: (b, 0, 0)),
                  pl.BlockSpec((1, s, d), lambda b, r: (b, 0, 2)),
                  pl.BlockSpec((1, n_ctx, d), lambda b, r: (b, 0, 0)),
                  pl.BlockSpec((1, n_ctx, d), lambda b, r: (b, 0, 2)),
                  pl.BlockSpec((1, NA_HEADS, GRID_W, wr * GRID_W), variant)],
        out_specs=pl.BlockSpec((1, GRID_W, d), lambda b, r: (b, r, 0)),
        compiler_params=_params("parallel", "arbitrary"), name="na_attention",
    )(q3, k3, qkv3, kc3, qkvc3, bias)
    oc = pl.pallas_call(
        _na_ctx_kernel, out_shape=jax.ShapeDtypeStruct((batch, n_ctx, d), BF16), grid=(batch,),
        in_specs=[pl.BlockSpec((1, n_ctx, d), lambda b: (b, 0, 0)),
                  pl.BlockSpec((1, n_ctx, d), lambda b: (b, 0, 0)),
                  pl.BlockSpec((1, n_ctx, d), lambda b: (b, 0, 2))],
        out_specs=pl.BlockSpec((1, n_ctx, d), lambda b: (b, 0, 0)),
        compiler_params=_params("parallel"), name="na_ctx_attention",
    )(qc3, kc3, qkvc3)
    return ol.reshape(batch * s, d), oc.reshape(batch * n_ctx, d)


def _diff_lambda(lam_ref, lambda_init):
    lv = lam_ref[...]
    a = jnp.sum(lv[0:1] * lv[1:2], axis=-1, keepdims=True)
    b = jnp.sum(lv[2:3] * lv[3:4], axis=-1, keepdims=True)
    return jnp.exp(a) - jnp.exp(b) + lambda_init


def _diff_out(o, onorm_ref, lambda_init, o_ref):
    y = o * lax.rsqrt(jnp.mean(o * o, axis=-1, keepdims=True) + RMS_EPS) * onorm_ref[...]
    o_ref[0] = (y * (1.0 - lambda_init)).astype(o_ref.dtype)


def _diff_kernel(q_ref, k_ref, v_ref, kc_ref, vc_ref, lam_ref, onorm_ref, o_ref, *, lambda_init):
    lam = _diff_lambda(lam_ref, lambda_init)
    q = q_ref[0]
    k, v, kc, vc = k_ref[0], v_ref[0], kc_ref[0], vc_ref[0]
    low = lax.broadcasted_iota(jnp.int32, q.shape, 1) < DIFF_HEAD_DIM
    a_loc = a_ctx = None
    for sub in range(2):
        qm = jnp.where(low if sub == 0 else ~low, q, jnp.zeros_like(q))
        s_loc = _dot_t(qm, k)
        s_ctx = _dot_t(qm, kc)
        m = jnp.maximum(jnp.max(s_loc, axis=-1, keepdims=True), jnp.max(s_ctx, axis=-1, keepdims=True))
        p_loc = jnp.exp(s_loc - m)
        p_ctx = jnp.exp(s_ctx - m)
        z = jnp.sum(p_loc, axis=-1, keepdims=True) + jnp.sum(p_ctx, axis=-1, keepdims=True)
        w = (1.0 / z) if sub == 0 else (-lam / z)
        a_loc = p_loc * w if sub == 0 else a_loc + p_loc * w
        a_ctx = p_ctx * w if sub == 0 else a_ctx + p_ctx * w
    o = _dot(a_loc.astype(BF16), v) + _dot(a_ctx.astype(BF16), vc)
    _diff_out(o, onorm_ref, lambda_init, o_ref)


def _diff_ctx_kernel(q_ref, k_ref, v_ref, lam_ref, onorm_ref, o_ref, *, lambda_init):
    lam = _diff_lambda(lam_ref, lambda_init)
    q, k, v = q_ref[0], k_ref[0], v_ref[0]
    tq = q.shape[0]
    low = lax.broadcasted_iota(jnp.int32, q.shape, 1) < DIFF_HEAD_DIM
    zero = jnp.zeros_like(q)
    qq = jnp.concatenate([jnp.where(low, q, zero), jnp.where(low, zero, q)], axis=0)
    s = _dot_t(qq, k)
    p = jnp.exp(s - jnp.max(s, axis=-1, keepdims=True))
    pv = _dot(p.astype(BF16), v) * (1.0 / jnp.sum(p, axis=-1, keepdims=True))
    _diff_out(pv[:tq] - lam * pv[tq:], onorm_ref, lambda_init, o_ref)


def _diff_attention(q, k, qkv, qc, kc, qkv_c, lam_vecs, out_norm, lambda_init, batch, tq=256):
    d = D_MODEL
    hw = 2 * DIFF_HEAD_DIM
    s = q.shape[0] // batch
    n_ctx = qc.shape[0] // batch
    v_blk = 2 * d // hw
    q3, k3, qkv3 = q.reshape(batch, s, d), k.reshape(batch, s, d), qkv.reshape(batch, s, 3 * d)
    qc3, kc3, qkvc3 = qc.reshape(batch, n_ctx, d), kc.reshape(batch, n_ctx, d), qkv_c.reshape(batch, n_ctx, 3 * d)
    onorm = out_norm.reshape(1, hw)
    ol = pl.pallas_call(
        functools.partial(_diff_kernel, lambda_init=lambda_init),
        out_shape=jax.ShapeDtypeStruct((batch, s, d), BF16), grid=(batch, DIFF_HEADS, s // tq),
        in_specs=[pl.BlockSpec((1, tq, hw), lambda b, h, i: (b, i, h)),
                  pl.BlockSpec((1, s, hw), lambda b, h, i: (b, 0, h)),
                  pl.BlockSpec((1, s, hw), lambda b, h, i: (b, 0, v_blk + h)),
                  pl.BlockSpec((1, n_ctx, hw), lambda b, h, i: (b, 0, h)),
                  pl.BlockSpec((1, n_ctx, hw), lambda b, h, i: (b, 0, v_blk + h)),
                  pl.BlockSpec((4, DIFF_HEAD_DIM), lambda b, h, i: (0, 0)),
                  pl.BlockSpec((1, hw), lambda b, h, i: (0, 0))],
        out_specs=pl.BlockSpec((1, tq, hw), lambda b, h, i: (b, i, h)),
        compiler_params=_params("parallel", "arbitrary", "arbitrary"), name="diff_attention",
    )(q3, k3, qkv3, kc3, qkvc3, lam_vecs, onorm)
    oc = pl.pallas_call(
        functools.partial(_diff_ctx_kernel, lambda_init=lambda_init),
        out_shape=jax.ShapeDtypeStruct((batch, n_ctx, d), BF16), grid=(batch, DIFF_HEADS),
        in_specs=[pl.BlockSpec((1, n_ctx, hw), lambda b, h: (b, 0, h)),
                  pl.BlockSpec((1, n_ctx, hw), lambda b, h: (b, 0, h)),
                  pl.BlockSpec((1, n_ctx, hw), lambda b, h: (b, 0, v_blk + h)),
                  pl.BlockSpec((4, DIFF_HEAD_DIM), lambda b, h: (0, 0)),
                  pl.BlockSpec((1, hw), lambda b, h: (0, 0))],
        out_specs=pl.BlockSpec((1, n_ctx, hw), lambda b, h: (b, 0, h)),
        compiler_params=_params("parallel", "arbitrary"), name="diff_ctx_attention",
    )(qc3, kc3, qkvc3, lam_vecs, onorm)
    return ol.reshape(batch * s, d), oc.reshape(batch * n_ctx, d)


def _conv_kernel(x_ref, w_ref, o_ref, *, act, n_norm_q, n_norm, qscale):
    x = x_ref[0].astype(F32)
    seq = x.shape[0]
    row = lax.broadcasted_iota(jnp.int32, x.shape, 0)
    prev = jnp.where(row == 0, 0.0, pltpu.roll(x, 1, 0))
    nxt = jnp.where(row == seq - 1, 0.0, pltpu.roll(x, seq - 1, 0))
    w = w_ref[...]
    y = prev * w[0:1] + x * w[1:2] + nxt * w[2:3]
    if act:
        y = _silu(y)
    if n_norm:
        j = pl.program_id(1)
        parts = []
        for g in range(y.shape[1] // LANES):
            seg = y[:, g * LANES:(g + 1) * LANES]
            inv = lax.rsqrt(jnp.sum(seg * seg, axis=-1, keepdims=True) + L2_EPS)
            scale = jnp.where(j < n_norm_q, inv * qscale, jnp.where(j < n_norm, inv, 1.0))
            parts.append(seg * scale)
        y = jnp.concatenate(parts, axis=-1) if len(parts) > 1 else parts[0]
    o_ref[0] = y.astype(o_ref.dtype)


def _dwconv3(x, w, batch, act=False, l2norm_cols=0, qscale=1.0, tn=256):
    m, c = x.shape
    seq = m // batch
    n_norm = l2norm_cols // tn
    out = pl.pallas_call(
        functools.partial(_conv_kernel, act=act, n_norm_q=n_norm // 2, n_norm=n_norm, qscale=qscale),
        out_shape=jax.ShapeDtypeStruct((batch, seq, c), BF16), grid=(batch, c // tn),
        in_specs=[pl.BlockSpec((1, seq, tn), lambda b, j: (b, 0, j)),
                  pl.BlockSpec((3, tn), lambda b, j: (0, j))],
        out_specs=pl.BlockSpec((1, seq, tn), lambda b, j: (b, 0, j)),
        compiler_params=_params("parallel", "arbitrary"), name="dwconv3",
    )(x.reshape(batch, seq, c), w)
    return out.reshape(m, c)


def _gdn_gate_kernel(h_ref, w_ref, wt_ref, a_ref, at_ref, bias_ref, biast_ref, col_ref, row_ref):
    h = h_ref[...]
    col = _dot(h, w_ref[...])
    row = _dot_t(wt_ref[...], h)

    def act(z, neg_a, bias, is_g):
        zb = z + bias
        softplus = jnp.maximum(zb, 0.0) + jnp.log(1.0 + jnp.exp(-jnp.abs(zb)))
        return jnp.where(is_g, neg_a * softplus, 1.0 / (1.0 + jnp.exp(-z)))

    lane = lax.broadcasted_iota(jnp.int32, col.shape, 1)
    col_ref[...] = act(col, a_ref[...], bias_ref[...], lane < 2 * GDN_HEADS)
    sub = lax.broadcasted_iota(jnp.int32, row.shape, 0)
    row_ref[...] = act(row, at_ref[...], biast_ref[...], sub < 2 * GDN_HEADS)


def _gdn_gates(h, w_beta, w_decay, a_log, dt_bias, tm=512):
    m, d = h.shape
    nh = GDN_HEADS
    tm = min(tm, m)
    w = jnp.concatenate([w_decay[0], w_decay[1], w_beta[0], w_beta[1]], axis=-1)
    wpad = jnp.zeros((d, LANES), F32).at[:, :4 * nh].set(w).astype(BF16)
    wt = w.T.astype(BF16)
    neg_a = jnp.concatenate([-jnp.exp(a_log[0]), -jnp.exp(a_log[1]), jnp.zeros((2 * nh,), F32)])
    bias = jnp.concatenate([dt_bias[0], dt_bias[1], jnp.zeros((2 * nh,), F32)])
    pad = lambda v: jnp.zeros((1, LANES), F32).at[0, :4 * nh].set(v)
    return pl.pallas_call(
        _gdn_gate_kernel,
        out_shape=(jax.ShapeDtypeStruct((m, LANES), F32), jax.ShapeDtypeStruct((4 * nh, m), F32)),
        grid=(m // tm,),
        in_specs=[pl.BlockSpec((tm, d), lambda i: (i, 0)),
                  pl.BlockSpec((d, LANES), lambda i: (0, 0)),
                  pl.BlockSpec((4 * nh, d), lambda i: (0, 0)),
                  pl.BlockSpec((1, LANES), lambda i: (0, 0)),
                  pl.BlockSpec((4 * nh, 1), lambda i: (0, 0)),
                  pl.BlockSpec((1, LANES), lambda i: (0, 0)),
                  pl.BlockSpec((4 * nh, 1), lambda i: (0, 0))],
        out_specs=(pl.BlockSpec((tm, LANES), lambda i: (i, 0)), pl.BlockSpec((4 * nh, tm), lambda i: (0, i))),
        compiler_params=_params("parallel"), name="gdn_gates",
    )(h, wpad, wt, pad(neg_a), neg_a.reshape(4 * nh, 1), pad(bias), bias.reshape(4 * nh, 1))


def _gdn_local_kernel(q_ref, k_ref, v_ref, col_ref, row_ref,
                      u_ref, w_ref, qg_ref, kd_ref, att_ref, dl_ref):
    c = GDN_CHUNK
    nh = GDN_HEADS
    hd = GDN_HEAD_DIM
    ii = lax.broadcasted_iota(jnp.int32, (c, c), 0)
    jj = lax.broadcasted_iota(jnp.int32, (c, c), 1)
    eye = (ii == jj).astype(F32)
    incl = [ii >= jj, ii <= jj]
    strict = [ii > jj, ii < jj]
    col = col_ref[0]
    row = row_ref[0, 0]
    gc_col, gc_row = [], []
    for d in range(2):
        m_col = incl[d].astype(F32)
        m_row = incl[1 - d].astype(F32)
        gc_col.append(jnp.dot(m_col, col[:, d * nh:(d + 1) * nh], preferred_element_type=F32, precision=HIGHEST))
        gc_row.append(jnp.dot(row[d * nh:(d + 1) * nh], m_row, preferred_element_type=F32, precision=HIGHEST))
    base = GDN_SOLVE_BASE
    same_base = (ii // base) == (jj // base)
    a_mats, rhs, tails = [], [], []
    for h in range(nh):
        cs = slice(h * hd, (h + 1) * hd)
        q, k, v = q_ref[0, :, cs], k_ref[0, :, cs], v_ref[0, :, cs]
        qf, kf, vf = q.astype(F32), k.astype(F32), v.astype(F32)
        kk = _dot_t(k, k)
        qk = _dot_t(q, k)
        for d in range(2):
            gcc = gc_col[d][:, h:h + 1]
            gcr = gc_row[d][h:h + 1, :]
            beta = col[:, 2 * nh + d * nh + h:2 * nh + d * nh + h + 1]
            dec = jnp.exp(jnp.where(incl[d], gcc - gcr, NEG_BIG))
            a_mats.append(jnp.where(strict[d], beta * kk * dec, 0.0))
            eg = jnp.exp(gcc)
            rhs.append(jnp.concatenate([vf * beta, kf * (beta * eg)], axis=-1).astype(BF16))
            g_last = gcr[:, c - 1:c] if d == 0 else gcr[:, 0:1]
            qg_ref[d, 0, :, cs] = (qf * eg).astype(qg_ref.dtype)
            kd_ref[d, 0, :, cs] = (kf * jnp.exp(g_last - gcc)).astype(kd_ref.dtype)
            att_ref[d, 0, 0, h] = (qk * dec).astype(att_ref.dtype)
            dl_ref[d, 0, 0, h:h + 1, :] = jnp.broadcast_to(jnp.exp(g_last), (1, hd))
            tails.append((d, cs))
    bdot = lambda x, y: _dot(x.astype(BF16), y.astype(BF16))
    npow = [jnp.where(same_base, -a, 0.0) for a in a_mats]
    inv = [eye + n for n in npow]
    span = 2
    while span < base:
        npow = [bdot(n, n) for n in npow]
        inv = [p + bdot(p, n) for p, n in zip(inv, npow)]
        span *= 2
    size = base
    while size < c:
        merge = jnp.logical_and((ii // (2 * size)) == (jj // (2 * size)), (ii // size) != (jj // size))
        low = [bdot(jnp.where(merge, a, 0.0), p) for a, p in zip(a_mats, inv)]
        inv = [p - bdot(p, x) for p, x in zip(inv, low)]
        size *= 2
    for p, r, (d, cs) in zip(inv, rhs, tails):
        sol = _dot(p.astype(BF16), r)
        u_ref[d, 0, :, cs] = sol[:, :hd]
        w_ref[d, 0, :, cs] = sol[:, hd:].astype(w_ref.dtype)


def _gdn_scan_kernel(uf_ref, wf_ref, qgf_ref, kdf_ref, attf_ref, dlf_ref,
                     ub_ref, wb_ref, qgb_ref, kdb_ref, attb_ref, dlb_ref, s0_ref,
                     of_ref, ob_ref, s_ref):
    hd = GDN_HEAD_DIM

    @pl.when(pl.program_id(1) == 0)
    def _():
        s_ref[...] = s0_ref[...]

    dirs = ((uf_ref, wf_ref, qgf_ref, kdf_ref, attf_ref, dlf_ref, of_ref),
            (ub_ref, wb_ref, qgb_ref, kdb_ref, attb_ref, dlb_ref, ob_ref))
    pairs = [(d, h) for d in range(2) for h in range(GDN_HEADS)]
    cols = lambda h: slice(h * hd, (h + 1) * hd)
    states = [s_ref[0, d, h] for d, h in pairs]
    sbs = [s.astype(BF16) for s in states]
    v_new = [dirs[d][0][0, 0, :, cols(h)] - _dot(dirs[d][1][0, 0, :, cols(h)], sb)
             for (d, h), sb in zip(pairs, sbs)]
    o_state = [_dot(dirs[d][2][0, 0, :, cols(h)], sb) for (d, h), sb in zip(pairs, sbs)]
    vbs = [v.astype(BF16) for v in v_new]
    for (d, h), o1, vb, s in zip(pairs, o_state, vbs, states):
        dirs[d][6][0, :, cols(h)] = o1 + _dot(dirs[d][4][0, 0, 0, h], vb)
        upd = lax.dot_general(dirs[d][3][0, 0, :, cols(h)], vb, (((0,), (0,)), ((), ())),
                              preferred_element_type=F32)
        s_ref[0, d, h] = s * dirs[d][5][0, 0, 0, h:h + 1, :] + upd


def _gdn_core(u3, gcol, grow, s0, batch):
    c = GDN_CHUNK
    nh, hd = GDN_HEADS, GDN_HEAD_DIM
    d = nh * hd
    m = u3.shape[0]
    seq = m // batch
    n = seq // c
    u33 = u3.reshape(batch, seq, 3 * d)
    gcol3 = gcol.reshape(batch, seq, LANES)
    grow4 = jnp.transpose(grow.reshape(4 * nh, batch, n, c), (1, 2, 0, 3))
    big = lambda dt: jax.ShapeDtypeStruct((2, batch, seq, d), dt)
    blk = pl.BlockSpec((2, 1, c, d), lambda b, i: (0, b, i, 0))
    u, w, qg, kd, att, dl = pl.pallas_call(
        _gdn_local_kernel,
        out_shape=(big(F32), big(BF16), big(BF16), big(BF16),
                   jax.ShapeDtypeStruct((2, batch, n, nh, c, c), BF16),
                   jax.ShapeDtypeStruct((2, batch, n, nh, hd), F32)),
        grid=(batch, n),
        in_specs=[pl.BlockSpec((1, c, d), lambda b, i: (b, i, 0)),
                  pl.BlockSpec((1, c, d), lambda b, i: (b, i, 1)),
                  pl.BlockSpec((1, c, d), lambda b, i: (b, i, 2)),
                  pl.BlockSpec((1, c, LANES), lambda b, i: (b, i, 0)),
                  pl.BlockSpec((1, 1, 4 * nh, c), lambda b, i: (b, i, 0, 0))],
        out_specs=(blk, blk, blk, blk,
                   pl.BlockSpec((2, 1, 1, nh, c, c), lambda b, i: (0, b, i, 0, 0, 0)),
                   pl.BlockSpec((2, 1, 1, nh, hd), lambda b, i: (0, b, i, 0, 0))),
        compiler_params=_params("parallel", "arbitrary"), name="gdn_local",
    )(u33, u33, u33, gcol3, grow4)

    def dir_specs(dd):
        pos = (lambda i: i) if dd == 0 else (lambda i: n - 1 - i)
        big_blk = pl.BlockSpec((1, 1, c, d), lambda b, i: (dd, b, pos(i), 0))
        return [big_blk, big_blk, big_blk, big_blk,
                pl.BlockSpec((1, 1, 1, nh, c, c), lambda b, i: (dd, b, pos(i), 0, 0, 0)),
                pl.BlockSpec((1, 1, 1, nh, hd), lambda b, i: (dd, b, pos(i), 0, 0))]

    s_blk = pl.BlockSpec((1, 2, nh, hd, hd), lambda b, i: (b, 0, 0, 0, 0))
    o_f, o_b, s_fin = pl.pallas_call(
        _gdn_scan_kernel,
        out_shape=(jax.ShapeDtypeStruct((batch, seq, d), F32), jax.ShapeDtypeStruct((batch, seq, d), F32),
                   jax.ShapeDtypeStruct((batch, 2, nh, hd, hd), F32)),
        grid=(batch, n),
        in_specs=dir_specs(0) + dir_specs(1) + [s_blk],
        out_specs=(pl.BlockSpec((1, c, d), lambda b, i: (b, i, 0)),
                   pl.BlockSpec((1, c, d), lambda b, i: (b, n - 1 - i, 0)), s_blk),
        compiler_params=_params("parallel", "arbitrary"), name="gdn_scan",
    )(u, w, qg, kd, att, dl, u, w, qg, kd, att, dl, s0)
    return o_f.reshape(m, d), o_b.reshape(m, d), s_fin


def _gdn_out_kernel(of_ref, ob_ref, gate_ref, norm_ref, o_ref):
    o = of_ref[...] + ob_ref[...]
    gate = _silu(gate_ref[...].astype(F32))
    parts = []
    for h in range(GDN_HEADS):
        seg = o[:, h * GDN_HEAD_DIM:(h + 1) * GDN_HEAD_DIM]
        parts.append(seg * lax.rsqrt(jnp.mean(seg * seg, axis=-1, keepdims=True) + RMS_EPS) * norm_ref[...])
    o_ref[...] = (jnp.concatenate(parts, axis=-1) * gate).astype(o_ref.dtype)


def _gdn_out(o_f, o_b, gate_lin, out_norm, tm=512):
    m, d = o_f.shape
    tm = min(tm, m)
    blk = pl.BlockSpec((tm, d), lambda i: (i, 0))
    return pl.pallas_call(
        _gdn_out_kernel, out_shape=jax.ShapeDtypeStruct((m, d), BF16), grid=(m // tm,),
        in_specs=[blk, blk, blk, pl.BlockSpec((1, GDN_HEAD_DIM), lambda i: (0, 0))],
        out_specs=blk, compiler_params=_params("parallel"), name="gdn_out",
    )(o_f, o_b, gate_lin, out_norm.reshape(1, GDN_HEAD_DIM))


def _gdn_branch(h, batch, s0, p):
    d = GDN_HEADS * GDN_HEAD_DIM
    lin = _matmul(h, p['w_qkv'])
    u3 = _dwconv3(lin, p['conv'], batch, act=True, l2norm_cols=2 * d, qscale=GDN_HEAD_DIM ** -0.5)
    gcol, grow = _gdn_gates(h, p['w_beta'], p['w_decay'], p['a_log'], p['dt_bias'])
    o_f, o_b, s_fin = _gdn_core(u3, gcol, grow, s0, batch)
    gate_lin = _matmul(h, p['w_gate'])
    return _gdn_out(o_f, o_b, gate_lin, p['out_norm']), s_fin


def _hy_filter_kernel(z_ref, w1_ref, b1_ref, fr_ref, w2_ref, b2_ref, w3f_ref, b3f_ref, w3b_ref, b3b_ref,
                      dl_ref, hf_ref, hb_ref):
    hdot = functools.partial(jnp.dot, preferred_element_type=F32, precision=HIGHEST)
    z = z_ref[...]
    fr = fr_ref[...]
    h1 = jnp.sin(fr[0:1] * (hdot(z, w1_ref[...]) + b1_ref[...]))
    h2 = jnp.sin(fr[1:2] * (hdot(h1, w2_ref[...]) + b2_ref[...]))
    decay = jnp.exp(-z[:, 0:1] * dl_ref[...])
    hf = (hdot(h2, w3f_ref[...]) + b3f_ref[...]) * decay
    hb = (hdot(h2, w3b_ref[...]) + b3b_ref[...]) * decay
    inv = 1.0 / (jnp.sum(jnp.abs(hf), axis=0, keepdims=True) + jnp.sum(jnp.abs(hb), axis=0, keepdims=True))
    hf_ref[0] = hf * inv
    hb_ref[0] = hb * inv


def _hyena_filters(seq, p, tn=256):
    d = D_MODEL
    width = p['w2'].shape[0]
    t = np.linspace(0.0, 1.0, seq)[:, None]
    bands = (HY_EMB_DIM - 1) // 2
    ang = (2.0 * math.pi * np.arange(seq) / seq)[:, None] * np.linspace(1e-4, bands - 1, bands)[None, :]
    feats = np.zeros((seq, LANES), np.float32)
    feats[:, :HY_EMB_DIM] = np.concatenate([t, np.cos(ang), -np.sin(ang)], axis=-1)
    w1 = jnp.zeros((LANES, width), F32).at[:HY_EMB_DIM].set(p['w1'])
    deltas = np.abs(np.linspace(HY_MIN_DECAY, HY_MAX_DECAY, d)).astype(np.float32)[None, :]
    nj = d // tn
    full = lambda shape: pl.BlockSpec(shape, lambda n, j: (0,) * len(shape))
    w3 = lambda dd: pl.BlockSpec((width, tn), lambda n, j: (0, (2 * n + dd) * nj + j))
    b3 = lambda dd: pl.BlockSpec((1, tn), lambda n, j: (0, (2 * n + dd) * nj + j))
    out = jax.ShapeDtypeStruct((HY_ORDER, seq, d), F32)
    oblk = pl.BlockSpec((1, seq, tn), lambda n, j: (n, 0, j))
    b3row = p['b3'].reshape(1, -1)
    return pl.pallas_call(
        _hy_filter_kernel, out_shape=(out, out), grid=(HY_ORDER, nj),
        in_specs=[full((seq, LANES)), full((LANES, width)), full((1, width)), full((2, width)),
                  full((width, width)), full((1, width)), w3(0), b3(0), w3(1), b3(1),
                  pl.BlockSpec((1, tn), lambda n, j: (0, j))],
        out_specs=(oblk, oblk), compiler_params=_params("arbitrary", "arbitrary"), name="hyena_filters",
    )(jnp.asarray(feats), w1, p['b1'].reshape(1, width), p['freq'], p['w2'], p['b2'].reshape(1, width),
      p['w3'], b3row, p['w3'], b3row, jnp.asarray(deltas))


def _dft_matrices(seq):
    f = lax.broadcasted_iota(jnp.int32, (seq, seq), 0)
    t = lax.broadcasted_iota(jnp.int32, (seq, seq), 1)
    ang = ((f * t) % (2 * seq)).astype(F32) * (math.pi / seq)
    nyq = (1 - 2 * (t % 2)).astype(F32)
    fwd = jnp.stack([jnp.cos(ang), jnp.where(f == 0, nyq, -jnp.sin(ang))])
    wgt = jnp.where(lax.broadcasted_iota(jnp.int32, (1, 1, seq), 2) == 0, 0.5 / seq, 1.0 / seq)
    inv = jnp.transpose(fwd, (0, 2, 1)) * wgt
    return fwd.astype(BF16), inv.astype(BF16)


def _hy_spectrum_kernel(f_ref, hf_ref, hb_ref, hr_ref, hi_ref, t_ref):
    hf = hf_ref[0]
    row = lax.broadcasted_iota(jnp.int32, hf.shape, 0)
    hb = jnp.where(row == 0, 0.0, hb_ref[0])
    hs = (hf + hb).astype(BF16)
    hd = (hf - hb).astype(BF16)
    hr = _dot(f_ref[0], hs)
    hi = _dot(f_ref[1], hd)
    nyq = _dot(f_ref[1, 0:8, :], hs)[0:1]
    orow = lax.broadcasted_iota(jnp.int32, hr.shape, 0)
    first = jnp.logical_and(pl.program_id(0) == 0, orow == 0)
    hr_ref[0] = hr
    hi_ref[0] = jnp.where(first, 0.0, hi)
    t_ref[0] = jnp.where(first, nyq, hr)


def _hyena_spectrum(hf, hb, fwd, fm=1024, tn=256):
    n_ord, seq, d = hf.shape
    fm = min(fm, seq)
    out = jax.ShapeDtypeStruct((n_ord, seq, d), F32)
    hblk = pl.BlockSpec((1, seq, tn), lambda c, n, j: (n, 0, j))
    oblk = pl.BlockSpec((1, fm, tn), lambda c, n, j: (n, c, j))
    return pl.pallas_call(
        _hy_spectrum_kernel, out_shape=(out, out, out), grid=(seq // fm, n_ord, d // tn),
        in_specs=[pl.BlockSpec((2, fm, seq), lambda c, n, j: (0, c, 0)), hblk, hblk],
        out_specs=(oblk, oblk, oblk),
        compiler_params=_params("arbitrary", "arbitrary", "arbitrary"), name="hyena_spectrum",
    )(fwd, hf, hb)


def _hy_fwd_kernel(f_ref, z_ref, hr_ref, hi_ref, t_ref, y_ref):
    u = z_ref[0]
    xr = _dot(f_ref[0], u)
    xi = _dot(f_ref[1], u)
    hr, hi, tt = hr_ref[0], hi_ref[0], t_ref[0]
    y_ref[0, 0] = (xr * hr - xi * hi).astype(y_ref.dtype)
    y_ref[0, 1] = (xr * hi + xi * tt).astype(y_ref.dtype)


def _hyena_fwd(z3, z_col0, fwd, hr, hi, tt, order, fm=1024, tn=256):
    batch, seq, _ = z3.shape
    d = D_MODEL
    fm = min(fm, seq)
    zoff = z_col0 // tn
    hblk = pl.BlockSpec((1, fm, tn), lambda c, b, j: (order, c, j))
    return pl.pallas_call(
        _hy_fwd_kernel, out_shape=jax.ShapeDtypeStruct((batch, 2, seq, d), BF16),
        grid=(seq // fm, batch, d // tn),
        in_specs=[pl.BlockSpec((2, fm, seq), lambda c, b, j: (0, c, 0)),
                  pl.BlockSpec((1, seq, tn), lambda c, b, j: (b, 0, zoff + j)), hblk, hblk, hblk],
        out_specs=pl.BlockSpec((1, 2, fm, tn), lambda c, b, j: (b, 0, c, j)),
        compiler_params=_params("arbitrary", "arbitrary", "arbitrary"), name="hyena_fwd_dft",
    )(fwd, z3, hr, hi, tt)


def _hy_inv_kernel(g_ref, y_ref, z_ref, gate_ref, skip_ref, o_ref):
    y = _dot(g_ref[0], y_ref[0, 0]) + _dot(g_ref[1], y_ref[0, 1])
    conv = y + z_ref[0].astype(F32) * skip_ref[...]
    o_ref[0] = (gate_ref[0].astype(F32) * conv).astype(o_ref.dtype)


def _hyena_inv(y, inv, z3, z_col0, gate3, gate_col0, skip, tmc=1024, tn=256):
    batch, _, seq, d = y.shape
    tmc = min(tmc, seq)
    zoff, goff = z_col0 // tn, gate_col0 // tn
    return pl.pallas_call(
        _hy_inv_kernel, out_shape=jax.ShapeDtypeStruct((batch, seq, d), BF16),
        grid=(seq // tmc, batch, d // tn),
        in_specs=[pl.BlockSpec((2, tmc, seq), lambda c, b, j: (0, c, 0)),
                  pl.BlockSpec((1, 2, seq, tn), lambda c, b, j: (b, 0, 0, j)),
                  pl.BlockSpec((1, tmc, tn), lambda c, b, j: (b, c, zoff + j)),
                  pl.BlockSpec((1, tmc, tn), lambda c, b, j: (b, c, goff + j)),
                  pl.BlockSpec((1, tn), lambda c, b, j: (0, j))],
        out_specs=pl.BlockSpec((1, tmc, tn), lambda c, b, j: (b, c, j)),
        compiler_params=_params("arbitrary", "arbitrary", "arbitrary"), name="hyena_inv_dft",
    )(inv, y, z3, gate3, skip.reshape(1, d))


def _hyena_branch(h, batch, p):
    d = D_MODEL
    m = h.shape[0]
    seq = m // batch
    xs = _dwconv3(_matmul(h, p['w_in']), p['conv'], batch).reshape(batch, seq, 3 * d)
    hf, hb = _hyena_filters(seq, p)
    fwd, inv = _dft_matrices(seq)
    hr, hi, tt = _hyena_spectrum(hf, hb, fwd)
    z, z_col0 = xs, 2 * d
    for n in range(HY_ORDER):
        y = _hyena_fwd(z, z_col0, fwd, hr, hi, tt, n)
        z, z_col0 = _hyena_inv(y, inv, z, z_col0, xs, n * d, p['skip'][n]), 0
    return z.reshape(m, d)


DENSE_FFN_PARTS = 2


def kernel(x, c, ctx, c_ctx, ada_w, ada_b, norm_g, na_w_qkv, na_q_norm, na_k_norm, na_rpb, na_w_o, gdn_w_qkv, gdn_conv, gdn_w_gate, gdn_w_beta, gdn_w_decay, gdn_a_log, gdn_dt_bias, gdn_out_norm, gdn_w_o, diff_w_qkv, diff_q_norm, diff_k_norm, diff_lambda, diff_out_norm, diff_w_o, hy_w_in, hy_conv, hy_filt_w1, hy_filt_b1, hy_filt_freq, hy_filt_w2, hy_filt_b2, hy_filt_w3, hy_filt_b3, hy_skip, hy_w_o, ffn_w_in, ffn_w_out, moe_router, moe_w_in, moe_w_out):
    batch, seq, d = x.shape
    n_ctx = ctx.shape[1]
    depth = ada_w.shape[0]
    bf = lambda w: w.astype(BF16)

    n_rows = -(-(batch + 1) // 8) * 8
    cc = jnp.zeros((n_rows, d), F32).at[:batch].set(c).at[batch].set(c_ctx)
    mods = _ada_mods(cc, ada_w, ada_b)

    xl = x.reshape(batch * seq, d)
    xc = ctx.reshape(batch * n_ctx, d)
    mods_l = [[mods[i, :batch, k * d:(k + 1) * d].reshape(batch, 1, d) for k in range(6)] for i in range(depth)]
    mods_c = [[mods[i, batch:batch + 1, k * d:(k + 1) * d].reshape(1, 1, d) for k in range(6)]
              for i in range(depth)]
    ctx_used = [i != depth - 1 or i % 4 != 3 for i in range(depth)]
    hl = _norm_modulate(xl, norm_g[0, 0], mods_l[0][0], mods_l[0][1])
    hc = _norm_modulate(xc, norm_g[0, 0], mods_c[0][0], mods_c[0][1]) if ctx_used[0] else None
    for i in range(depth):
        last = i == depth - 1
        kind = i % 4
        ml, mc = mods_l[i], mods_c[i]
        ctx_needed = ctx_used[i]
        oc = None
        if kind == 0:
            w = bf(na_w_qkv)
            qkv, qkv_c = _matmul(hl, w), _matmul(hc, w)
            gq = jnp.tile(na_q_norm, NA_HEADS) * NA_HEAD_DIM ** -0.5
            gk = jnp.tile(na_k_norm, NA_HEADS)
            ol, oc = _neighbourhood_attention(
                _head_norm(qkv, 0, gq, NA_HEAD_DIM), _head_norm(qkv, 1, gk, NA_HEAD_DIM), qkv,
                _head_norm(qkv_c, 0, gq, NA_HEAD_DIM), _head_norm(qkv_c, 1, gk, NA_HEAD_DIM), qkv_c,
                na_rpb, batch)
            w_o = bf(na_w_o)
        elif kind == 1:
            p = dict(w_qkv=bf(gdn_w_qkv), conv=gdn_conv, w_gate=bf(gdn_w_gate), w_beta=gdn_w_beta,
                     w_decay=gdn_w_decay, a_log=gdn_a_log, dt_bias=gdn_dt_bias, out_norm=gdn_out_norm)
            zeros = jnp.zeros((batch, 2, GDN_HEADS, GDN_HEAD_DIM, GDN_HEAD_DIM), F32)
            oc, s_ctx = _gdn_branch(hc, batch, zeros, p)
            ol, _ = _gdn_branch(hl, batch, s_ctx, p)
            w_o = bf(gdn_w_o)
        elif kind == 2:
            lambda_init = 0.8 - 0.6 * math.exp(-0.3 * i)
            w = bf(diff_w_qkv)
            qkv, qkv_c = _matmul(hl, w), _matmul(hc, w)
            reps = d // DIFF_HEAD_DIM
            gq = jnp.tile(diff_q_norm, reps) * DIFF_HEAD_DIM ** -0.5
            gk = jnp.tile(diff_k_norm, reps)
            ol, oc = _diff_attention(
                _head_norm(qkv, 0, gq, DIFF_HEAD_DIM, seq=seq, rope=True),
                _head_norm(qkv, 1, gk, DIFF_HEAD_DIM, seq=seq, rope=True), qkv,
                _head_norm(qkv_c, 0, gq, DIFF_HEAD_DIM), _head_norm(qkv_c, 1, gk, DIFF_HEAD_DIM), qkv_c,
                diff_lambda, diff_out_norm, lambda_init, batch)
            w_o = bf(diff_w_o)
        else:
            p = dict(w_in=bf(hy_w_in), conv=hy_conv, w1=hy_filt_w1, b1=hy_filt_b1, freq=hy_filt_freq,
                     w2=hy_filt_w2, b2=hy_filt_b2, w3=hy_filt_w3, b3=hy_filt_b3, skip=hy_skip)
            ol = _hyena_branch(hl, batch, p)
            oc = _hyena_branch(hc, batch, p) if ctx_needed else None
            w_o = bf(hy_w_o)
        j = i // 2
        dense = i % 2 == 0
        streams = [(xl, ol, ml, None if last else mods_l[i + 1])]
        if not last:
            streams.append((xc, oc, mc, mods_c[i + 1] if ctx_used[i + 1] else None))
        outs = []
        for xs, os_, ms, ms_next in streams:
            nxt = None if ms_next is None else (norm_g[i + 1, 0], ms_next[0], ms_next[1])
            if dense:
                xs, hs = _matmul_residual(os_, w_o, xs, ms[2], norm=(norm_g[i, 1], ms[3], ms[4]))
                w_in = bf(ffn_w_in[j]).reshape(1, d, -1)
                w_out = bf(ffn_w_out[j]).reshape(DENSE_FFN_PARTS, -1, d)
                res = _dense_ffn(hs, w_in, w_out, xs, ms[5], norm=nxt)
            else:
                xs = _matmul_residual(os_, w_o, xs, ms[2])
                res = _moe(xs, norm_g[i, 1], ms[3], ms[4], ms[5], moe_router[j],
                           bf(moe_w_in[j]), bf(moe_w_out[j]), norm=nxt)
            outs.append(res if nxt is not None else (res, None))
        xl, hl = outs[0]
        if not last:
            xc, hc = outs[1]
    return xl.reshape(batch, seq, d)
```

```python
import functools
import math

import numpy as np
import jax
import jax.numpy as jnp
from jax import lax
from jax.experimental import pallas as pl
from jax.experimental.pallas import tpu as pltpu

F32 = jnp.float32
BF16 = jnp.bfloat16
HIGHEST = lax.Precision.HIGHEST

VMEM_LIMIT_BYTES = 56 * 1024 * 1024
LANES = 128

D_MODEL = 1024
GRID_W = 64
RMS_EPS = 1e-6
L2_EPS = 1e-6
ROPE_BASE = 10000.0
NA_HEADS = 16
NA_HEAD_DIM = 64
NA_WIN_ROWS = 8
NA_WIN_COLS = 16
GDN_HEADS = 8
GDN_HEAD_DIM = 128
GDN_CHUNK = 64
GDN_SOLVE_BASE = 8
SCAN_BATCH_GROUP = 2
DIFF_HEADS = 8
DIFF_HEAD_DIM = 64
HY_ORDER = 2
HY_EMB_DIM = 33
HY_MAX_DECAY = math.log(1e-2) / 0.3
HY_MIN_DECAY = math.log(1e-2) / 1.5
MOE_EXPERTS = 8
NEG_BIG = -1e30


def _params(*sem):
    return pltpu.CompilerParams(dimension_semantics=sem, vmem_limit_bytes=VMEM_LIMIT_BYTES)


def _dot(a, b):
    return jnp.dot(a, b, preferred_element_type=F32)


def _dot_t(a, b):
    return lax.dot_general(a, b, (((1,), (1,)), ((), ())), preferred_element_type=F32)


def _silu(x):
    return x * (1.0 / (1.0 + jnp.exp(-x)))


def _ada_kernel(c_ref, w_ref, b_ref, o_ref):
    s = _silu(c_ref[...])
    o_ref[0] = jnp.dot(s, w_ref[0], preferred_element_type=F32, precision=HIGHEST) + b_ref[0]


def _ada_mods(cc, ada_w, ada_b):
    depth, d, n = ada_w.shape
    r = cc.shape[0]
    tn = 1024
    return pl.pallas_call(
        _ada_kernel,
        out_shape=jax.ShapeDtypeStruct((depth, r, n), F32),
        grid=(depth, n // tn),
        in_specs=[pl.BlockSpec((r, d), lambda i, j: (0, 0)),
                  pl.BlockSpec((1, d, tn), lambda i, j: (i, 0, j)),
                  pl.BlockSpec((1, 1, tn), lambda i, j: (i, 0, j))],
        out_specs=pl.BlockSpec((1, r, tn), lambda i, j: (i, 0, j)),
        compiler_params=_params("arbitrary", "arbitrary"),
        name="ada_mods",
    )(cc, ada_w, ada_b.reshape(depth, 1, n))


def _norm_mod(x, g_ref, sh_ref, sc_ref):
    y = x * lax.rsqrt(jnp.mean(x * x, axis=-1, keepdims=True) + RMS_EPS) * g_ref[...]
    return y * (1.0 + sc_ref[0]) + sh_ref[0]


def _norm_mod_kernel(x_ref, g_ref, sh_ref, sc_ref, o_ref):
    o_ref[...] = _norm_mod(x_ref[...], g_ref, sh_ref, sc_ref).astype(o_ref.dtype)


def _norm_specs(d, rows, tm, grid_rank):
    if grid_rank == 1:
        mod = pl.BlockSpec((1, 1, d), lambda i, *_: (i * tm // rows, 0, 0))
        return [pl.BlockSpec((1, d), lambda i, *_: (0, 0)), mod, mod]
    mod = pl.BlockSpec((1, 1, d), lambda i, j, *_: (i * tm // rows, 0, 0))
    return [pl.BlockSpec((1, d), lambda i, j, *_: (0, 0)), mod, mod]


ROW_GROUPS = D_MODEL // LANES


def _store_token_major(ref, x):
    n = x.shape[0]
    for g in range(ROW_GROUPS):
        ref[pl.ds(g, n, stride=ROW_GROUPS), :] = x[:, g * LANES:(g + 1) * LANES]


def _load_token_major(ref, n, lead=()):
    return jnp.concatenate([ref[lead + (pl.ds(g, n, stride=ROW_GROUPS), slice(None))]
                            for g in range(ROW_GROUPS)], axis=1)


def _norm_mod_route_kernel(x_ref, g_ref, sh_ref, sc_ref, r_ref, o_ref, info_ref, cnt_ref):
    h = _norm_mod(x_ref[...], g_ref, sh_ref, sc_ref)
    _store_token_major(o_ref, h)
    logits = jnp.dot(h, r_ref[...], preferred_element_type=F32, precision=HIGHEST)
    lane = lax.broadcasted_iota(jnp.int32, logits.shape, 1).astype(F32)
    l1 = jnp.where(lane < MOE_EXPERTS, logits, NEG_BIG)
    m1 = jnp.max(l1, axis=-1, keepdims=True)
    i1 = jnp.min(jnp.where(l1 == m1, lane, float(LANES)), axis=-1, keepdims=True)
    l2 = jnp.where(lane == i1, NEG_BIG, l1)
    m2 = jnp.max(l2, axis=-1, keepdims=True)
    i2 = jnp.min(jnp.where(l2 == m2, lane, float(LANES)), axis=-1, keepdims=True)
    e2 = jnp.exp(m2 - m1)
    w1 = 1.0 / (1.0 + e2)
    info_ref[...] = (jnp.where(lane == 0.0, i1, 0.0) + jnp.where(lane == 1.0, i2, 0.0)
                     + jnp.where(lane == 2.0, w1, 0.0) + jnp.where(lane == 3.0, e2 * w1, 0.0))
    hit = jnp.logical_or(lane == i1, lane == i2).astype(F32)
    cnt_ref[0] = jnp.sum(hit, axis=0, keepdims=True)


def _norm_modulate(x, g, shift, scale, router=None, tm=512):
    m, d = x.shape
    grp = shift.shape[0]
    rows = m // grp
    tm = min(tm, rows)
    assert rows % tm == 0
    in_specs = [pl.BlockSpec((tm, d), lambda i: (i, 0)),
                pl.BlockSpec((1, d), lambda i: (0, 0)),
                pl.BlockSpec((1, 1, d), lambda i: (i * tm // rows, 0, 0)),
                pl.BlockSpec((1, 1, d), lambda i: (i * tm // rows, 0, 0))]
    args = [x, g.reshape(1, d), shift, scale]
    if router is None:
        return pl.pallas_call(
            _norm_mod_kernel, out_shape=jax.ShapeDtypeStruct((m, d), BF16), grid=(m // tm,),
            in_specs=in_specs, out_specs=pl.BlockSpec((tm, d), lambda i: (i, 0)),
            compiler_params=_params("parallel"), name="norm_mod")(*args)
    rpad = jnp.zeros((d, LANES), F32).at[:, :router.shape[1]].set(router)
    assert tm == ROUTE_TILE
    return pl.pallas_call(
        _norm_mod_route_kernel,
        out_shape=(jax.ShapeDtypeStruct((m * ROW_GROUPS, LANES), F32), jax.ShapeDtypeStruct((m, LANES), F32),
                   jax.ShapeDtypeStruct((m // tm, 1, LANES), F32)),
        grid=(m // tm,),
        in_specs=in_specs + [pl.BlockSpec((d, LANES), lambda i: (0, 0))],
        out_specs=(pl.BlockSpec((tm * ROW_GROUPS, LANES), lambda i: (i, 0)),
                   pl.BlockSpec((tm, LANES), lambda i: (i, 0)),
                   pl.BlockSpec((1, 1, LANES), lambda i: (i, 0, 0))),
        compiler_params=_params("parallel"), name="norm_mod_route")(*args, rpad)


def _mm_kernel(x_ref, w_ref, o_ref):
    o_ref[...] = _dot(x_ref[...], w_ref[...]).astype(o_ref.dtype)


def _mm_res_kernel(x_ref, w_ref, res_ref, gate_ref, o_ref):
    o_ref[...] = res_ref[...] + gate_ref[0] * _dot(x_ref[...], w_ref[...])


def _mm_res_norm_kernel(x_ref, w_ref, res_ref, gate_ref, g_ref, sh_ref, sc_ref, o_ref, h_ref):
    y = res_ref[...] + gate_ref[0] * _dot(x_ref[...], w_ref[...])
    o_ref[...] = y
    h_ref[...] = _norm_mod(y, g_ref, sh_ref, sc_ref).astype(h_ref.dtype)


def _matmul(x, w, out_dtype=BF16, tm=1024, tn=1024):
    m, k = x.shape
    n = w.shape[1]
    tm, tn = min(tm, m), min(tn, n)
    assert m % tm == 0 and n % tn == 0
    return pl.pallas_call(
        _mm_kernel, out_shape=jax.ShapeDtypeStruct((m, n), out_dtype), grid=(m // tm, n // tn),
        in_specs=[pl.BlockSpec((tm, k), lambda i, j: (i, 0)), pl.BlockSpec((k, tn), lambda i, j: (0, j))],
        out_specs=pl.BlockSpec((tm, tn), lambda i, j: (i, j)),
        compiler_params=_params("parallel", "arbitrary"), name="matmul")(x, w)


def _matmul_residual(x, w, res, gate, norm=None, tm=1024, tn=1024):
    m, k = x.shape
    n = w.shape[1]
    rows = m // gate.shape[0]
    tm, tn = min(tm, rows), min(tn, n)
    assert rows % tm == 0 and n % tn == 0
    if norm is not None:
        assert tn == n
        blk = pl.BlockSpec((tm, n), lambda i, j: (i, 0))
        return pl.pallas_call(
            _mm_res_norm_kernel,
            out_shape=(jax.ShapeDtypeStruct((m, n), F32), jax.ShapeDtypeStruct((m, n), BF16)), grid=(m // tm, 1),
            in_specs=[pl.BlockSpec((tm, k), lambda i, j: (i, 0)), pl.BlockSpec((k, n), lambda i, j: (0, 0)), blk,
                      pl.BlockSpec((1, 1, n), lambda i, j: (i * tm // rows, 0, 0))] + _norm_specs(n, rows, tm, 2),
            out_specs=(blk, blk), compiler_params=_params("parallel", "arbitrary"), name="matmul_residual_norm",
        )(x, w, res, gate, norm[0].reshape(1, n), norm[1], norm[2])
    return pl.pallas_call(
        _mm_res_kernel, out_shape=jax.ShapeDtypeStruct((m, n), F32), grid=(m // tm, n // tn),
        in_specs=[pl.BlockSpec((tm, k), lambda i, j: (i, 0)), pl.BlockSpec((k, tn), lambda i, j: (0, j)),
                  pl.BlockSpec((tm, tn), lambda i, j: (i, j)),
                  pl.BlockSpec((1, 1, tn), lambda i, j: (i * tm // rows, 0, j))],
        out_specs=pl.BlockSpec((tm, tn), lambda i, j: (i, j)),
        compiler_params=_params("parallel", "arbitrary"), name="matmul_residual")(x, w, res, gate)


def _swiglu(x, wa_ref, wb_ref, wo_ref):
    a = _dot(x, wa_ref[0])
    b = _dot(x, wb_ref[0])
    return _dot((_silu(a) * b).astype(BF16), wo_ref[0])


def _ffn_kernel(*refs, with_norm):
    if with_norm:
        h_ref, wa_ref, wb_ref, wo_ref, res_ref, mod_ref, g_ref, sh_ref, sc_ref, o_ref, hn_ref, acc_ref = refs
    else:
        h_ref, wa_ref, wb_ref, wo_ref, res_ref, mod_ref, o_ref, acc_ref = refs
    e = pl.program_id(1)
    y = _swiglu(h_ref[...], wa_ref, wb_ref, wo_ref)

    @pl.when(e == 0)
    def _():
        acc_ref[...] = y

    @pl.when(e > 0)
    def _():
        acc_ref[...] += y

    @pl.when(e == pl.num_programs(1) - 1)
    def _():
        out = res_ref[...] + mod_ref[0] * acc_ref[...]
        o_ref[...] = out
        if with_norm:
            hn_ref[...] = _norm_mod(out, g_ref, sh_ref, sc_ref).astype(hn_ref.dtype)


def _dense_ffn(h, w_in, w_out, res, mod, norm=None, tm=512):
    m, d = h.shape
    rows = m // mod.shape[0]
    tm = min(tm, rows)
    assert rows % tm == 0
    parts, f, _ = w_out.shape
    blk = pl.BlockSpec((tm, d), lambda i, e: (i, 0))
    in_specs = [blk,
                pl.BlockSpec((1, d, f), lambda i, e: (0, 0, e)),
                pl.BlockSpec((1, d, f), lambda i, e: (0, 0, parts + e)),
                pl.BlockSpec((1, f, d), lambda i, e: (e, 0, 0)),
                blk,
                pl.BlockSpec((1, 1, d), lambda i, e: (i * tm // rows, 0, 0))]
    args = [h, w_in, w_in, w_out, res, mod]
    out_shape, out_specs = jax.ShapeDtypeStruct((m, d), F32), blk
    if norm is not None:
        in_specs += _norm_specs(d, rows, tm, 2)
        args += [norm[0].reshape(1, d), norm[1], norm[2]]
        out_shape, out_specs = (out_shape, jax.ShapeDtypeStruct((m, d), BF16)), (blk, blk)
    return pl.pallas_call(
        functools.partial(_ffn_kernel, with_norm=norm is not None),
        out_shape=out_shape, grid=(m // tm, parts), in_specs=in_specs, out_specs=out_specs,
        scratch_shapes=[pltpu.VMEM((tm, d), F32)],
        compiler_params=_params("parallel", "arbitrary"), name="ffn")(*args)


ROUTE_TILE = 512
EXPERT_TILE = 512
DMA_CHUNK = 256


def _route_plan_kernel(cnt_ref, base_ref, te_ref, used_ref):
    hdot = functools.partial(jnp.dot, preferred_element_type=F32, precision=HIGHEST)
    cnt = cnt_ref[...]
    nt = cnt.shape[0]
    tot = jnp.sum(cnt, axis=0, keepdims=True)
    seg = jnp.floor((tot + (EXPERT_TILE - 1)) * (1.0 / EXPERT_TILE)) * EXPERT_TILE
    e_r = lax.broadcasted_iota(jnp.int32, (LANES, LANES), 0)
    e_c = lax.broadcasted_iota(jnp.int32, (LANES, LANES), 1)
    off = hdot(jnp.broadcast_to(seg, (8, LANES)), (e_r < e_c).astype(F32))[0:1]
    t_r = lax.broadcasted_iota(jnp.int32, (nt, nt), 0)
    t_c = lax.broadcasted_iota(jnp.int32, (nt, nt), 1)
    base_ref[...] = hdot((t_c < t_r).astype(F32), cnt) + off
    ends = off + seg
    start = lax.broadcasted_iota(jnp.int32, te_ref.shape, 0).astype(F32) * EXPERT_TILE
    lane = lax.broadcasted_iota(jnp.int32, te_ref.shape, 1)
    done = jnp.where(jnp.logical_and(lane < MOE_EXPERTS, start >= ends), 1.0, 0.0)
    te = jnp.minimum(jnp.sum(done, axis=-1, keepdims=True), MOE_EXPERTS - 1.0)
    te_ref[...] = jnp.broadcast_to(te, te_ref.shape).astype(jnp.int32)
    used = ends[:, MOE_EXPERTS - 1:MOE_EXPERTS] * (1.0 / EXPERT_TILE)
    used_ref[...] = jnp.broadcast_to(used, used_ref.shape).astype(jnp.int32)


def _route_pos_kernel(info_ref, base_ref, pos_ref):
    info = info_ref[...]
    lane = lax.broadcasted_iota(jnp.int32, info.shape, 1).astype(F32)
    hit1 = lane == info[:, 0:1]
    hit2 = lane == info[:, 1:2]
    tm = info.shape[0]
    r_i = lax.broadcasted_iota(jnp.int32, (tm, tm), 0)
    r_j = lax.broadcasted_iota(jnp.int32, (tm, tm), 1)
    earlier = jnp.where(r_j < r_i, 1.0, 0.0).astype(BF16)
    rank = _dot(earlier, jnp.where(jnp.logical_or(hit1, hit2), 1.0, 0.0).astype(BF16))
    p = rank + base_ref[0]
    pos1 = jnp.sum(jnp.where(hit1, p, 0.0), axis=-1, keepdims=True)
    pos2 = jnp.sum(jnp.where(hit2, p, 0.0), axis=-1, keepdims=True)
    pos_ref[...] = jnp.where(lane == 0.0, pos1, jnp.where(lane == 1.0, pos2, 0.0)).astype(jnp.int32)


ROW_COPY_UNROLL = 8


def _row_copies(n_rows, copies_of_row, whole_buffer_copies):
    def start(g, carry):
        for u in range(ROW_COPY_UNROLL):
            for cp in copies_of_row(g * ROW_COPY_UNROLL + u):
                cp.start()
        return carry

    lax.fori_loop(0, n_rows // ROW_COPY_UNROLL, start, 0)
    for cp in whole_buffer_copies:
        cp.wait()


def _row_token_kernel(pos_ref, tok_ref, *, n_tokens):
    def clear(r, carry):
        tok_ref[r] = 0
        return carry

    def put(t, carry):
        tok_ref[pos_ref[t]] = t
        tok_ref[pos_ref[n_tokens + t]] = t
        return carry

    lax.fori_loop(0, tok_ref.shape[0], clear, 0, unroll=8)
    lax.fori_loop(0, n_tokens, put, 0, unroll=8)


def _gather_kernel(tok_ref, used_ref, h_ref, xs_ref, buf, sem):
    first = pl.program_id(0) * EXPERT_TILE
    live = pl.program_id(0) < used_ref[0]

    @pl.when(live)
    def _():
        g = ROW_GROUPS
        _row_copies(EXPERT_TILE,
                    lambda j: [pltpu.make_async_copy(
                        h_ref.at[pl.ds(pl.multiple_of(tok_ref[first + j] * g, g), g)],
                        buf.at[pl.ds(pl.multiple_of(j * g, g), g)], sem)],
                    [pltpu.make_async_copy(h_ref.at[pl.ds(0, EXPERT_TILE * g)], buf, sem)])
        xs_ref[...] = _load_token_major(buf, EXPERT_TILE).astype(xs_ref.dtype)

    @pl.when(jnp.logical_not(live))
    def _():
        xs_ref[...] = jnp.zeros_like(xs_ref)


def _ffn_routed_kernel(te_ref, used_ref, x_ref, wa_ref, wb_ref, wo_ref, y_ref):
    live = pl.program_id(0) < used_ref[0]

    @pl.when(live)
    def _():
        _store_token_major(y_ref, _swiglu(x_ref[...], wa_ref, wb_ref, wo_ref))

    @pl.when(jnp.logical_not(live))
    def _():
        y_ref[...] = jnp.zeros_like(y_ref)


def _combine_kernel(*refs, n_tokens, with_norm):
    if with_norm:
        pos_ref, ys_ref, res_ref, info_ref, mod_ref, g_ref, sh_ref, sc_ref, o_ref, hn_ref, buf, sem = refs
    else:
        pos_ref, ys_ref, res_ref, info_ref, mod_ref, o_ref, buf, sem = refs
    first = pl.program_id(0) * DMA_CHUNK
    g = ROW_GROUPS
    _row_copies(DMA_CHUNK,
                lambda j: [pltpu.make_async_copy(
                    ys_ref.at[pl.ds(pl.multiple_of(pos_ref[k * n_tokens + first + j] * g, g), g)],
                    buf.at[k, pl.ds(pl.multiple_of(j * g, g), g)], sem) for k in range(2)],
                [pltpu.make_async_copy(ys_ref.at[pl.ds(0, DMA_CHUNK * g)], buf.at[k], sem) for k in range(2)])
    info = info_ref[...]
    y = (info[:, 2:3] * _load_token_major(buf, DMA_CHUNK, (0,))
         + info[:, 3:4] * _load_token_major(buf, DMA_CHUNK, (1,)))
    out = res_ref[...] + mod_ref[0] * y
    o_ref[...] = out
    if with_norm:
        hn_ref[...] = _norm_mod(out, g_ref, sh_ref, sc_ref).astype(hn_ref.dtype)


def _moe(x, g, shift, scale, mod, router, w_in, w_out, norm=None):
    m, d = x.shape
    n_e, _, two_f = w_in.shape
    f = two_f // 2
    rows = m // mod.shape[0]
    nt = m // ROUTE_TILE
    n_tiles = 2 * m // EXPERT_TILE + n_e
    n_sorted = n_tiles * EXPERT_TILE
    h32, info, cnt = _norm_modulate(x, g, shift, scale, router=router, tm=ROUTE_TILE)

    te_rows = -(-n_tiles // 8) * 8
    base, te, used = pl.pallas_call(
        _route_plan_kernel,
        out_shape=(jax.ShapeDtypeStruct((nt, LANES), F32), jax.ShapeDtypeStruct((te_rows, LANES), jnp.int32),
                   jax.ShapeDtypeStruct((8, LANES), jnp.int32)),
        name="route_plan")(cnt.reshape(nt, LANES))
    pos = pl.pallas_call(
        _route_pos_kernel, out_shape=jax.ShapeDtypeStruct((m, LANES), jnp.int32), grid=(nt,),
        in_specs=[pl.BlockSpec((ROUTE_TILE, LANES), lambda i: (i, 0)),
                  pl.BlockSpec((1, 1, LANES), lambda i: (i, 0, 0))],
        out_specs=pl.BlockSpec((ROUTE_TILE, LANES), lambda i: (i, 0)),
        compiler_params=_params("parallel"), name="route_pos")(info, base.reshape(nt, 1, LANES))
    pos_flat = jnp.concatenate([pos[:, 0], pos[:, 1]])
    te_flat, used_flat = te[:, 0], used[0, :1]

    any_spec = pl.BlockSpec(memory_space=pl.ANY)
    smem_spec = pl.BlockSpec(memory_space=pltpu.SMEM)
    row_token = pl.pallas_call(
        functools.partial(_row_token_kernel, n_tokens=m),
        out_shape=jax.ShapeDtypeStruct((n_sorted,), jnp.int32),
        in_specs=[smem_spec], out_specs=smem_spec, name="moe_row_token")(pos_flat)
    xs = pl.pallas_call(
        _gather_kernel,
        grid_spec=pltpu.PrefetchScalarGridSpec(
            num_scalar_prefetch=2, grid=(n_tiles,), in_specs=[any_spec],
            out_specs=pl.BlockSpec((EXPERT_TILE, d), lambda i, tok, used: (i, 0)),
            scratch_shapes=[pltpu.VMEM((EXPERT_TILE * ROW_GROUPS, LANES), F32), pltpu.SemaphoreType.DMA(())]),
        out_shape=jax.ShapeDtypeStruct((n_sorted, d), BF16),
        compiler_params=_params("arbitrary"), name="moe_gather",
    )(row_token, used_flat, h32)

    ys = pl.pallas_call(
        _ffn_routed_kernel,
        grid_spec=pltpu.PrefetchScalarGridSpec(
            num_scalar_prefetch=2, grid=(n_tiles,),
            in_specs=[pl.BlockSpec((EXPERT_TILE, d), lambda i, te, used: (i, 0)),
                      pl.BlockSpec((1, d, f), lambda i, te, used: (te[i], 0, 0)),
                      pl.BlockSpec((1, d, f), lambda i, te, used: (te[i], 0, 1)),
                      pl.BlockSpec((1, f, d), lambda i, te, used: (te[i], 0, 0))],
            out_specs=pl.BlockSpec((EXPERT_TILE * ROW_GROUPS, LANES), lambda i, te, used: (i, 0))),
        out_shape=jax.ShapeDtypeStruct((n_sorted * ROW_GROUPS, LANES), F32),
        compiler_params=_params("arbitrary"), name="moe_expert_ffn",
    )(te_flat, used_flat, xs, w_in, w_in, w_out)

    blk = pl.BlockSpec((DMA_CHUNK, d), lambda i, pos: (i, 0))
    in_specs = [any_spec, blk, pl.BlockSpec((DMA_CHUNK, LANES), lambda i, pos: (i, 0)),
                pl.BlockSpec((1, 1, d), lambda i, pos: (i * DMA_CHUNK // rows, 0, 0))]
    args = [pos_flat, ys, x, info, mod]
    out_shape, out_specs = jax.ShapeDtypeStruct((m, d), F32), blk
    if norm is not None:
        in_specs += _norm_specs(d, rows, DMA_CHUNK, 1)
        args += [norm[0].reshape(1, d), norm[1], norm[2]]
        out_shape, out_specs = (out_shape, jax.ShapeDtypeStruct((m, d), BF16)), (blk, blk)
    return pl.pallas_call(
        functools.partial(_combine_kernel, n_tokens=m, with_norm=norm is not None),
        grid_spec=pltpu.PrefetchScalarGridSpec(
            num_scalar_prefetch=1, grid=(m // DMA_CHUNK,), in_specs=in_specs, out_specs=out_specs,
            scratch_shapes=[pltpu.VMEM((2, DMA_CHUNK * ROW_GROUPS, LANES), F32), pltpu.SemaphoreType.DMA(())]),
        out_shape=out_shape, compiler_params=_params("arbitrary"), name="moe_combine",
    )(*args)


def _head_norm_kernel(*refs, group, rope):
    if rope:
        x_ref, gain_ref, ind_ref, ind_t_ref, cos_ref, sa_ref, sb_ref, o_ref = refs
    else:
        x_ref, gain_ref, ind_ref, ind_t_ref, o_ref = refs
    x = x_ref[...].astype(F32)
    ss = _dot((x * x).astype(BF16), ind_ref[...])
    inv = lax.rsqrt(ss * (1.0 / group) + RMS_EPS)
    inv_hi = inv.astype(BF16)
    inv_lo = (inv - inv_hi.astype(F32)).astype(BF16)
    y = x * (_dot(inv_hi, ind_t_ref[...]) + _dot(inv_lo, ind_t_ref[...])) * gain_ref[...]
    if rope:
        d = y.shape[-1]
        rep = d // LANES
        half = group // 4
        wide = lambda t_ref: jnp.concatenate([t_ref[...]] * rep, axis=1)
        y = (y * wide(cos_ref) + pltpu.roll(y, d - half, 1) * wide(sa_ref)
             + pltpu.roll(y, half, 1) * wide(sb_ref))
    o_ref[...] = y.astype(o_ref.dtype)


def _rope_tables(seq):
    pos = np.arange(seq)
    lane = np.arange(LANES) % DIFF_HEAD_DIM
    half = DIFF_HEAD_DIM // 2
    j = lane % half
    inv = ROPE_BASE ** (-(2.0 * (j % (half // 2))) / half)
    p = np.where(lane[None, :] < half, (pos // GRID_W)[:, None], (pos % GRID_W)[:, None])
    ang = p * inv[None, :]
    first = (j < half // 2)[None, :]
    cos, sin = np.cos(ang), np.sin(ang)
    return (jnp.asarray(cos, F32), jnp.asarray(np.where(first, -sin, 0.0), F32),
            jnp.asarray(np.where(first, 0.0, sin), F32))


def _head_norm(src, col_block, gain, group, seq=None, rope=False, tm=512):
    m = src.shape[0]
    d = D_MODEL
    tm = min(tm, m if seq is None else seq)
    assert m % tm == 0
    in_specs = [pl.BlockSpec((tm, d), lambda i: (i, col_block)),
                pl.BlockSpec((1, d), lambda i: (0, 0)),
                pl.BlockSpec((d, LANES), lambda i: (0, 0)),
                pl.BlockSpec((LANES, d), lambda i: (0, 0))]
    ind = (np.arange(d)[:, None] // group == np.arange(LANES)[None, :]).astype(np.float32)
    args = [src, gain.reshape(1, d), jnp.asarray(ind).astype(BF16), jnp.asarray(ind.T).astype(BF16)]
    if rope:
        nblk = seq // tm
        in_specs += [pl.BlockSpec((tm, LANES), lambda i: (i % nblk, 0))] * 3
        args += list(_rope_tables(seq))
    return pl.pallas_call(
        functools.partial(_head_norm_kernel, group=group, rope=rope),
        out_shape=jax.ShapeDtypeStruct((m, d), BF16), grid=(m // tm,),
        in_specs=in_specs, out_specs=pl.BlockSpec((tm, d), lambda i: (i, 0)),
        compiler_params=_params("parallel"), name="head_norm")(*args)


def _na_bias_table(rpb, rows):
    wr = min(NA_WIN_ROWS, rows)
    cols = np.arange(GRID_W)
    c0 = np.clip(cols - NA_WIN_COLS // 2, 0, GRID_W - NA_WIN_COLS)
    col_valid = (cols[None, :] >= c0[:, None]) & (cols[None, :] < c0[:, None] + NA_WIN_COLS)
    col_idx = np.clip(cols[None, :] - cols[:, None], 1 - NA_WIN_COLS, NA_WIN_COLS - 1) + NA_WIN_COLS - 1
    pick = jnp.asarray((col_idx[None] == np.arange(2 * NA_WIN_COLS - 1)[:, None, None]).astype(np.float32))
    lo = NA_WIN_ROWS - 1
    rows_of = jnp.stack([rpb[:, lo - off:lo - off + wr] for off in range(wr)])
    bias = jnp.einsum('ohic,cqk->ohqik', rows_of, pick, precision=HIGHEST)
    bias = jnp.where(col_valid[None, None, :, None, :], bias, NEG_BIG)
    return bias.reshape(wr, NA_HEADS, GRID_W, wr * GRID_W).astype(F32)


def _na_row_start(r, rows, wr):
    return jnp.clip(r - wr // 2, 0, rows - wr)


def _na_kernel(q_ref, k_ref, v_ref, kc_ref, vc_ref, bias_ref, o_ref, *, rows, wr):
    r = pl.program_id(1)
    start = pl.multiple_of(_na_row_start(r, rows, wr) * GRID_W, GRID_W)
    lane = lax.broadcasted_iota(jnp.int32, (GRID_W, LANES), 1)
    low = lane < NA_HEAD_DIM
    win = pl.ds(start, wr * GRID_W)
    cols = lambda h: slice((h // 2) * LANES, (h // 2 + 1) * LANES)
    heads = range(NA_HEADS)
    scores = []
    for h in heads:
        q = q_ref[0, :, cols(h)]
        qm = jnp.where(low if h % 2 == 0 else ~low, q, jnp.zeros_like(q))
        scores.append((_dot_t(qm, k_ref[0, win, cols(h)]) + bias_ref[0, h], _dot_t(qm, kc_ref[0, :, cols(h)])))
    probs = []
    for s_loc, s_ctx in scores:
        m = jnp.maximum(jnp.max(s_loc, axis=-1, keepdims=True), jnp.max(s_ctx, axis=-1, keepdims=True))
        p_loc = jnp.exp(s_loc - m)
        p_ctx = jnp.exp(s_ctx - m)
        z = jnp.sum(p_loc, axis=-1, keepdims=True) + jnp.sum(p_ctx, axis=-1, keepdims=True)
        probs.append((p_loc.astype(BF16), p_ctx.astype(BF16), 1.0 / z))
    outs = [(_dot(p_loc, v_ref[0, win, cols(h)]) + _dot(p_ctx, vc_ref[0, :, cols(h)])) * rz
            for h, (p_loc, p_ctx, rz) in zip(heads, probs)]
    for pair in range(NA_HEADS // 2):
        o_ref[0, :, cols(2 * pair)] = jnp.where(low, outs[2 * pair], outs[2 * pair + 1]).astype(o_ref.dtype)


def _na_ctx_kernel(q_ref, k_ref, v_ref, o_ref):
    lane = lax.broadcasted_iota(jnp.int32, (q_ref.shape[1], LANES), 1)
    low = lane < NA_HEAD_DIM
    for pair in range(NA_HEADS // 2):
        cs = slice(pair * LANES, (pair + 1) * LANES)
        q, k, v = q_ref[0, :, cs], k_ref[0, :, cs], v_ref[0, :, cs]
        outs = []
        for sub in range(2):
            qm = jnp.where(low if sub == 0 else ~low, q, jnp.zeros_like(q))
            s = _dot_t(qm, k)
            p = jnp.exp(s - jnp.max(s, axis=-1, keepdims=True))
            outs.append(_dot(p.astype(BF16), v) * (1.0 / jnp.sum(p, axis=-1, keepdims=True)))
        o_ref[0, :, cs] = jnp.where(low, outs[0], outs[1]).astype(o_ref.dtype)


def _neighbourhood_attention(q, k, qkv, qc, kc, qkv_c, rpb, batch):
    d = D_MODEL
    s = q.shape[0] // batch
    n_ctx = qc.shape[0] // batch
    rows = s // GRID_W
    wr = min(NA_WIN_ROWS, rows)
    bias = _na_bias_table(rpb, rows)
    q3, k3, qkv3 = q.reshape(batch, s, d), k.reshape(batch, s, d), qkv.reshape(batch, s, 3 * d)
    qc3, kc3, qkvc3 = qc.reshape(batch, n_ctx, d), kc.reshape(batch, n_ctx, d), qkv_c.reshape(batch, n_ctx, 3 * d)

    def variant(b, r):
        return (r - _na_row_start(r, rows, wr), 0, 0, 0)

    ol = pl.pallas_call(
        functools.partial(_na_kernel, rows=rows, wr=wr),
        out_shape=jax.ShapeDtypeStruct((batch, s, d), BF16), grid=(batch, rows),
        in_specs=[pl.BlockSpec((1, GRID_W, d), lambda b, r: (b, r, 0)),
                  pl.BlockSpec((1, s, d), lambda b, r: (b, 0, 0)),
                  pl.BlockSpec((1, s, d), lambda b, r: (b, 0, 2)),
                  pl.BlockSpec((1, n_ctx, d), lambda b, r: (b, 0, 0)),
                  pl.BlockSpec((1, n_ctx, d), lambda b, r: (b, 0, 2)),
                  pl.BlockSpec((1, NA_HEADS, GRID_W, wr * GRID_W), variant)],
        out_specs=pl.BlockSpec((1, GRID_W, d), lambda b, r: (b, r, 0)),
        compiler_params=_params("parallel", "arbitrary"), name="na_attention",
    )(q3, k3, qkv3, kc3, qkvc3, bias)
    oc = pl.pallas_call(
        _na_ctx_kernel, out_shape=jax.ShapeDtypeStruct((batch, n_ctx, d), BF16), grid=(batch,),
        in_specs=[pl.BlockSpec((1, n_ctx, d), lambda b: (b, 0, 0)),
                  pl.BlockSpec((1, n_ctx, d), lambda b: (b, 0, 0)),
                  pl.BlockSpec((1, n_ctx, d), lambda b: (b, 0, 2))],
        out_specs=pl.BlockSpec((1, n_ctx, d), lambda b: (b, 0, 0)),
        compiler_params=_params("parallel"), name="na_ctx_attention",
    )(qc3, kc3, qkvc3)
    return ol.reshape(batch * s, d), oc.reshape(batch * n_ctx, d)


def _diff_lambda(lam_ref, lambda_init):
    lv = lam_ref[...]
    a = jnp.sum(lv[0:1] * lv[1:2], axis=-1, keepdims=True)
    b = jnp.sum(lv[2:3] * lv[3:4], axis=-1, keepdims=True)
    return jnp.exp(a) - jnp.exp(b) + lambda_init


def _diff_out(o, onorm_ref, lambda_init, o_ref):
    y = o * lax.rsqrt(jnp.mean(o * o, axis=-1, keepdims=True) + RMS_EPS) * onorm_ref[...]
    o_ref[0] = (y * (1.0 - lambda_init)).astype(o_ref.dtype)


def _diff_kernel(q_ref, k_ref, v_ref, kc_ref, vc_ref, lam_ref, onorm_ref, o_ref, *, lambda_init):
    lam = _diff_lambda(lam_ref, lambda_init)
    q = q_ref[0]
    k, v, kc, vc = k_ref[0], v_ref[0], kc_ref[0], vc_ref[0]
    low = lax.broadcasted_iota(jnp.int32, q.shape, 1) < DIFF_HEAD_DIM
    a_loc = a_ctx = None
    for sub in range(2):
        qm = jnp.where(low if sub == 0 else ~low, q, jnp.zeros_like(q))
        s_loc = _dot_t(qm, k)
        s_ctx = _dot_t(qm, kc)
        m = jnp.maximum(jnp.max(s_loc, axis=-1, keepdims=True), jnp.max(s_ctx, axis=-1, keepdims=True))
        p_loc = jnp.exp(s_loc - m)
        p_ctx = jnp.exp(s_ctx - m)
        z = jnp.sum(p_loc, axis=-1, keepdims=True) + jnp.sum(p_ctx, axis=-1, keepdims=True)
        w = (1.0 / z) if sub == 0 else (-lam / z)
        a_loc = p_loc * w if sub == 0 else a_loc + p_loc * w
        a_ctx = p_ctx * w if sub == 0 else a_ctx + p_ctx * w
    o = _dot(a_loc.astype(BF16), v) + _dot(a_ctx.astype(BF16), vc)
    _diff_out(o, onorm_ref, lambda_init, o_ref)


def _diff_ctx_kernel(q_ref, k_ref, v_ref, lam_ref, onorm_ref, o_ref, *, lambda_init):
    lam = _diff_lambda(lam_ref, lambda_init)
    q, k, v = q_ref[0], k_ref[0], v_ref[0]
    tq = q.shape[0]
    low = lax.broadcasted_iota(jnp.int32, q.shape, 1) < DIFF_HEAD_DIM
    zero = jnp.zeros_like(q)
    qq = jnp.concatenate([jnp.where(low, q, zero), jnp.where(low, zero, q)], axis=0)
    s = _dot_t(qq, k)
    p = jnp.exp(s - jnp.max(s, axis=-1, keepdims=True))
    pv = _dot(p.astype(BF16), v) * (1.0 / jnp.sum(p, axis=-1, keepdims=True))
    _diff_out(pv[:tq] - lam * pv[tq:], onorm_ref, lambda_init, o_ref)


def _diff_attention(q, k, qkv, qc, kc, qkv_c, lam_vecs, out_norm, lambda_init, batch, tq=256):
    d = D_MODEL
    hw = 2 * DIFF_HEAD_DIM
    s = q.shape[0] // batch
    n_ctx = qc.shape[0] // batch
    v_blk = 2 * d // hw
    q3, k3, qkv3 = q.reshape(batch, s, d), k.reshape(batch, s, d), qkv.reshape(batch, s, 3 * d)
    qc3, kc3, qkvc3 = qc.reshape(batch, n_ctx, d), kc.reshape(batch, n_ctx, d), qkv_c.reshape(batch, n_ctx, 3 * d)
    onorm = out_norm.reshape(1, hw)
    ol = pl.pallas_call(
        functools.partial(_diff_kernel, lambda_init=lambda_init),
        out_shape=jax.ShapeDtypeStruct((batch, s, d), BF16), grid=(batch, DIFF_HEADS, s // tq),
        in_specs=[pl.BlockSpec((1, tq, hw), lambda b, h, i: (b, i, h)),
                  pl.BlockSpec((1, s, hw), lambda b, h, i: (b, 0, h)),
                  pl.BlockSpec((1, s, hw), lambda b, h, i: (b, 0, v_blk + h)),
                  pl.BlockSpec((1, n_ctx, hw), lambda b, h, i: (b, 0, h)),
                  pl.BlockSpec((1, n_ctx, hw), lambda b, h, i: (b, 0, v_blk + h)),
                  pl.BlockSpec((4, DIFF_HEAD_DIM), lambda b, h, i: (0, 0)),
                  pl.BlockSpec((1, hw), lambda b, h, i: (0, 0))],
        out_specs=pl.BlockSpec((1, tq, hw), lambda b, h, i: (b, i, h)),
        compiler_params=_params("parallel", "arbitrary", "arbitrary"), name="diff_attention",
    )(q3, k3, qkv3, kc3, qkvc3, lam_vecs, onorm)
    oc = pl.pallas_call(
        functools.partial(_diff_ctx_kernel, lambda_init=lambda_init),
        out_shape=jax.ShapeDtypeStruct((batch, n_ctx, d), BF16), grid=(batch, DIFF_HEADS),
        in_specs=[pl.BlockSpec((1, n_ctx, hw), lambda b, h: (b, 0, h)),
                  pl.BlockSpec((1, n_ctx, hw), lambda b, h: (b, 0, h)),
                  pl.BlockSpec((1, n_ctx, hw), lambda b, h: (b, 0, v_blk + h)),
                  pl.BlockSpec((4, DIFF_HEAD_DIM), lambda b, h: (0, 0)),
                  pl.BlockSpec((1, hw), lambda b, h: (0, 0))],
        out_specs=pl.BlockSpec((1, n_ctx, hw), lambda b, h: (b, 0, h)),
        compiler_params=_params("parallel", "arbitrary"), name="diff_ctx_attention",
    )(qc3, kc3, qkvc3, lam_vecs, onorm)
    return ol.reshape(batch * s, d), oc.reshape(batch * n_ctx, d)


def _conv_kernel(x_ref, w_ref, o_ref, *, act, n_norm_q, n_norm, qscale):
    x = x_ref[0].astype(F32)
    seq = x.shape[0]
    row = lax.broadcasted_iota(jnp.int32, x.shape, 0)
    prev = jnp.where(row == 0, 0.0, pltpu.roll(x, 1, 0))
    nxt = jnp.where(row == seq - 1, 0.0, pltpu.roll(x, seq - 1, 0))
    w = w_ref[...]
    y = prev * w[0:1] + x * w[1:2] + nxt * w[2:3]
    if act:
        y = _silu(y)
    if n_norm:
        j = pl.program_id(1)
        parts = []
        for g in range(y.shape[1] // LANES):
            seg = y[:, g * LANES:(g + 1) * LANES]
            inv = lax.rsqrt(jnp.sum(seg * seg, axis=-1, keepdims=True) + L2_EPS)
            scale = jnp.where(j < n_norm_q, inv * qscale, jnp.where(j < n_norm, inv, 1.0))
            parts.append(seg * scale)
        y = jnp.concatenate(parts, axis=-1) if len(parts) > 1 else parts[0]
    o_ref[0] = y.astype(o_ref.dtype)


def _dwconv3(x, w, batch, act=False, l2norm_cols=0, qscale=1.0, tn=256):
    m, c = x.shape
    seq = m // batch
    n_norm = l2norm_cols // tn
    out = pl.pallas_call(
        functools.partial(_conv_kernel, act=act, n_norm_q=n_norm // 2, n_norm=n_norm, qscale=qscale),
        out_shape=jax.ShapeDtypeStruct((batch, seq, c), BF16), grid=(batch, c // tn),
        in_specs=[pl.BlockSpec((1, seq, tn), lambda b, j: (b, 0, j)),
                  pl.BlockSpec((3, tn), lambda b, j: (0, j))],
        out_specs=pl.BlockSpec((1, seq, tn), lambda b, j: (b, 0, j)),
        compiler_params=_params("parallel", "arbitrary"), name="dwconv3",
    )(x.reshape(batch, seq, c), w)
    return out.reshape(m, c)


def _gdn_gate_kernel(h_ref, w_ref, wt_ref, a_ref, at_ref, bias_ref, biast_ref, col_ref, row_ref):
    h = h_ref[...]
    col = _dot(h, w_ref[...])
    row = _dot_t(wt_ref[...], h)

    def act(z, neg_a, bias, is_g):
        zb = z + bias
        softplus = jnp.maximum(zb, 0.0) + jnp.log(1.0 + jnp.exp(-jnp.abs(zb)))
        return jnp.where(is_g, neg_a * softplus, 1.0 / (1.0 + jnp.exp(-z)))

    lane = lax.broadcasted_iota(jnp.int32, col.shape, 1)
    col_ref[...] = act(col, a_ref[...], bias_ref[...], lane < 2 * GDN_HEADS)
    sub = lax.broadcasted_iota(jnp.int32, row.shape, 0)
    row_ref[...] = act(row, at_ref[...], biast_ref[...], sub < 2 * GDN_HEADS)


def _gdn_gates(h, w_beta, w_decay, a_log, dt_bias, tm=512):
    m, d = h.shape
    nh = GDN_HEADS
    tm = min(tm, m)
    w = jnp.concatenate([w_decay[0], w_decay[1], w_beta[0], w_beta[1]], axis=-1)
    wpad = jnp.zeros((d, LANES), F32).at[:, :4 * nh].set(w).astype(BF16)
    wt = w.T.astype(BF16)
    neg_a = jnp.concatenate([-jnp.exp(a_log[0]), -jnp.exp(a_log[1]), jnp.zeros((2 * nh,), F32)])
    bias = jnp.concatenate([dt_bias[0], dt_bias[1], jnp.zeros((2 * nh,), F32)])
    pad = lambda v: jnp.zeros((1, LANES), F32).at[0, :4 * nh].set(v)
    return pl.pallas_call(
        _gdn_gate_kernel,
        out_shape=(jax.ShapeDtypeStruct((m, LANES), F32), jax.ShapeDtypeStruct((4 * nh, m), F32)),
        grid=(m // tm,),
        in_specs=[pl.BlockSpec((tm, d), lambda i: (i, 0)),
                  pl.BlockSpec((d, LANES), lambda i: (0, 0)),
                  pl.BlockSpec((4 * nh, d), lambda i: (0, 0)),
                  pl.BlockSpec((1, LANES), lambda i: (0, 0)),
                  pl.BlockSpec((4 * nh, 1), lambda i: (0, 0)),
                  pl.BlockSpec((1, LANES), lambda i: (0, 0)),
                  pl.BlockSpec((4 * nh, 1), lambda i: (0, 0))],
        out_specs=(pl.BlockSpec((tm, LANES), lambda i: (i, 0)), pl.BlockSpec((4 * nh, tm), lambda i: (0, i))),
        compiler_params=_params("parallel"), name="gdn_gates",
    )(h, wpad, wt, pad(neg_a), neg_a.reshape(4 * nh, 1), pad(bias), bias.reshape(4 * nh, 1))


def _gdn_local_kernel(q_ref, k_ref, v_ref, col_ref, row_ref,
                      u_ref, w_ref, qg_ref, kd_ref, att_ref, dl_ref):
    c = GDN_CHUNK
    nh = GDN_HEADS
    hd = GDN_HEAD_DIM
    ii = lax.broadcasted_iota(jnp.int32, (c, c), 0)
    jj = lax.broadcasted_iota(jnp.int32, (c, c), 1)
    eye = (ii == jj).astype(F32)
    incl = [ii >= jj, ii <= jj]
    strict = [ii > jj, ii < jj]
    col = col_ref[0]
    row = row_ref[0, 0]
    gc_col, gc_row = [], []
    for d in range(2):
        m_col = incl[d].astype(F32)
        m_row = incl[1 - d].astype(F32)
        gc_col.append(jnp.dot(m_col, col[:, d * nh:(d + 1) * nh], preferred_element_type=F32, precision=HIGHEST))
        gc_row.append(jnp.dot(row[d * nh:(d + 1) * nh], m_row, preferred_element_type=F32, precision=HIGHEST))
    base = GDN_SOLVE_BASE
    same_base = (ii // base) == (jj // base)
    a_mats, rhs, tails = [], [], []
    for h in range(nh):
        cs = slice(h * hd, (h + 1) * hd)
        q, k, v = q_ref[0, :, cs], k_ref[0, :, cs], v_ref[0, :, cs]
        qf, kf, vf = q.astype(F32), k.astype(F32), v.astype(F32)
        kk = _dot_t(k, k)
        qk = _dot_t(q, k)
        for d in range(2):
            gcc = gc_col[d][:, h:h + 1]
            gcr = gc_row[d][h:h + 1, :]
            beta = col[:, 2 * nh + d * nh + h:2 * nh + d * nh + h + 1]
            dec = jnp.exp(jnp.where(incl[d], gcc - gcr, NEG_BIG))
            a_mats.append(jnp.where(strict[d], beta * kk * dec, 0.0))
            eg = jnp.exp(gcc)
            rhs.append(jnp.concatenate([vf * beta, kf * (beta * eg)], axis=-1).astype(BF16))
            g_last = gcr[:, c - 1:c] if d == 0 else gcr[:, 0:1]
            qg_ref[d, 0, :, cs] = (qf * eg).astype(qg_ref.dtype)
            kd_ref[d, 0, :, cs] = (kf * jnp.exp(g_last - gcc)).astype(kd_ref.dtype)
            att_ref[d, 0, 0, h] = (qk * dec).astype(att_ref.dtype)
            dl_ref[d, 0, 0, h:h + 1, :] = jnp.broadcast_to(jnp.exp(g_last), (1, hd))
            tails.append((d, cs))
    bdot = lambda x, y: _dot(x.astype(BF16), y.astype(BF16))
    npow = [jnp.where(same_base, -a, 0.0) for a in a_mats]
    inv = [eye + n for n in npow]
    span = 2
    while span < base:
        npow = [bdot(n, n) for n in npow]
        inv = [p + bdot(p, n) for p, n in zip(inv, npow)]
        span *= 2
    size = base
    while size < c:
        merge = jnp.logical_and((ii // (2 * size)) == (jj // (2 * size)), (ii // size) != (jj // size))
        low = [bdot(jnp.where(merge, a, 0.0), p) for a, p in zip(a_mats, inv)]
        inv = [p - bdot(p, x) for p, x in zip(inv, low)]
        size *= 2
    for p, r, (d, cs) in zip(inv, rhs, tails):
        sol = _dot(p.astype(BF16), r)
        u_ref[d, 0, :, cs] = sol[:, :hd]
        w_ref[d, 0, :, cs] = sol[:, hd:].astype(w_ref.dtype)


def _gdn_scan_kernel(uf_ref, wf_ref, qgf_ref, kdf_ref, attf_ref, dlf_ref,
                     ub_ref, wb_ref, qgb_ref, kdb_ref, attb_ref, dlb_ref, s0_ref,
                     of_ref, ob_ref, s_ref):
    hd = GDN_HEAD_DIM

    @pl.when(pl.program_id(1) == 0)
    def _():
        s_ref[...] = s0_ref[...]

    dirs = ((uf_ref, wf_ref, qgf_ref, kdf_ref, attf_ref, dlf_ref, of_ref),
            (ub_ref, wb_ref, qgb_ref, kdb_ref, attb_ref, dlb_ref, ob_ref))
    pairs = [(g, d, h) for g in range(s_ref.shape[0]) for d in range(2) for h in range(GDN_HEADS)]
    cols = lambda h: slice(h * hd, (h + 1) * hd)
    states = [s_ref[g, d, h] for g, d, h in pairs]
    sbs = [s.astype(BF16) for s in states]
    v_new = [dirs[d][0][0, g, :, cols(h)] - _dot(dirs[d][1][0, g, :, cols(h)], sb)
             for (g, d, h), sb in zip(pairs, sbs)]
    o_state = [_dot(dirs[d][2][0, g, :, cols(h)], sb) for (g, d, h), sb in zip(pairs, sbs)]
    vbs = [v.astype(BF16) for v in v_new]
    for (g, d, h), o1, vb, s in zip(pairs, o_state, vbs, states):
        dirs[d][6][g, :, cols(h)] = o1 + _dot(dirs[d][4][0, g, 0, h], vb)
        upd = lax.dot_general(dirs[d][3][0, g, :, cols(h)], vb, (((0,), (0,)), ((), ())),
                              preferred_element_type=F32)
        s_ref[g, d, h] = s * dirs[d][5][0, g, 0, h:h + 1, :] + upd


def _gdn_core(u3, gcol, grow, s0, batch):
    c = GDN_CHUNK
    nh, hd = GDN_HEADS, GDN_HEAD_DIM
    d = nh * hd
    m = u3.shape[0]
    seq = m // batch
    n = seq // c
    u33 = u3.reshape(batch, seq, 3 * d)
    gcol3 = gcol.reshape(batch, seq, LANES)
    grow4 = jnp.transpose(grow.reshape(4 * nh, batch, n, c), (1, 2, 0, 3))
    big = lambda dt: jax.ShapeDtypeStruct((2, batch, seq, d), dt)
    blk = pl.BlockSpec((2, 1, c, d), lambda b, i: (0, b, i, 0))
    u, w, qg, kd, att, dl = pl.pallas_call(
        _gdn_local_kernel,
        out_shape=(big(F32), big(BF16), big(BF16), big(BF16),
                   jax.ShapeDtypeStruct((2, batch, n, nh, c, c), BF16),
                   jax.ShapeDtypeStruct((2, batch, n, nh, hd), F32)),
        grid=(batch, n),
        in_specs=[pl.BlockSpec((1, c, d), lambda b, i: (b, i, 0)),
                  pl.BlockSpec((1, c, d), lambda b, i: (b, i, 1)),
                  pl.BlockSpec((1, c, d), lambda b, i: (b, i, 2)),
                  pl.BlockSpec((1, c, LANES), lambda b, i: (b, i, 0)),
                  pl.BlockSpec((1, 1, 4 * nh, c), lambda b, i: (b, i, 0, 0))],
        out_specs=(blk, blk, blk, blk,
                   pl.BlockSpec((2, 1, 1, nh, c, c), lambda b, i: (0, b, i, 0, 0, 0)),
                   pl.BlockSpec((2, 1, 1, nh, hd), lambda b, i: (0, b, i, 0, 0))),
        compiler_params=_params("parallel", "arbitrary"), name="gdn_local",
    )(u33, u33, u33, gcol3, grow4)

    def dir_specs(dd):
        pos = (lambda i: i) if dd == 0 else (lambda i: n - 1 - i)
        big_blk = pl.BlockSpec((1, gb, c, d), lambda b, i: (dd, b, pos(i), 0))
        return [big_blk, big_blk, big_blk, big_blk,
                pl.BlockSpec((1, gb, 1, nh, c, c), lambda b, i: (dd, b, pos(i), 0, 0, 0)),
                pl.BlockSpec((1, gb, 1, nh, hd), lambda b, i: (dd, b, pos(i), 0, 0))]

    gb = SCAN_BATCH_GROUP if batch % SCAN_BATCH_GROUP == 0 else 1
    s_blk = pl.BlockSpec((gb, 2, nh, hd, hd), lambda b, i: (b, 0, 0, 0, 0))
    o_f, o_b, s_fin = pl.pallas_call(
        _gdn_scan_kernel,
        out_shape=(jax.ShapeDtypeStruct((batch, seq, d), F32), jax.ShapeDtypeStruct((batch, seq, d), F32),
                   jax.ShapeDtypeStruct((batch, 2, nh, hd, hd), F32)),
        grid=(batch // gb, n),
        in_specs=dir_specs(0) + dir_specs(1) + [s_blk],
        out_specs=(pl.BlockSpec((gb, c, d), lambda b, i: (b, i, 0)),
                   pl.BlockSpec((gb, c, d), lambda b, i: (b, n - 1 - i, 0)), s_blk),
        compiler_params=_params("parallel", "arbitrary"), name="gdn_scan",
    )(u, w, qg, kd, att, dl, u, w, qg, kd, att, dl, s0)
    return o_f.reshape(m, d), o_b.reshape(m, d), s_fin


def _gdn_out_kernel(of_ref, ob_ref, gate_ref, norm_ref, o_ref):
    o = of_ref[...] + ob_ref[...]
    gate = _silu(gate_ref[...].astype(F32))
    parts = []
    for h in range(GDN_HEADS):
        seg = o[:, h * GDN_HEAD_DIM:(h + 1) * GDN_HEAD_DIM]
        parts.append(seg * lax.rsqrt(jnp.mean(seg * seg, axis=-1, keepdims=True) + RMS_EPS) * norm_ref[...])
    o_ref[...] = (jnp.concatenate(parts, axis=-1) * gate).astype(o_ref.dtype)


def _gdn_out(o_f, o_b, gate_lin, out_norm, tm=512):
    m, d = o_f.shape
    tm = min(tm, m)
    blk = pl.BlockSpec((tm, d), lambda i: (i, 0))
    return pl.pallas_call(
        _gdn_out_kernel, out_shape=jax.ShapeDtypeStruct((m, d), BF16), grid=(m // tm,),
        in_specs=[blk, blk, blk, pl.BlockSpec((1, GDN_HEAD_DIM), lambda i: (0, 0))],
        out_specs=blk, compiler_params=_params("parallel"), name="gdn_out",
    )(o_f, o_b, gate_lin, out_norm.reshape(1, GDN_HEAD_DIM))


def _gdn_branch(h, batch, s0, p):
    d = GDN_HEADS * GDN_HEAD_DIM
    lin = _matmul(h, p['w_qkv'])
    u3 = _dwconv3(lin, p['conv'], batch, act=True, l2norm_cols=2 * d, qscale=GDN_HEAD_DIM ** -0.5)
    gcol, grow = _gdn_gates(h, p['w_beta'], p['w_decay'], p['a_log'], p['dt_bias'])
    o_f, o_b, s_fin = _gdn_core(u3, gcol, grow, s0, batch)
    gate_lin = _matmul(h, p['w_gate'])
    return _gdn_out(o_f, o_b, gate_lin, p['out_norm']), s_fin


def _hy_filter_kernel(z_ref, w1_ref, b1_ref, fr_ref, w2_ref, b2_ref, w3f_ref, b3f_ref, w3b_ref, b3b_ref,
                      dl_ref, hf_ref, hb_ref):
    hdot = functools.partial(jnp.dot, preferred_element_type=F32, precision=HIGHEST)
    z = z_ref[...]
    fr = fr_ref[...]
    h1 = jnp.sin(fr[0:1] * (hdot(z, w1_ref[...]) + b1_ref[...]))
    h2 = jnp.sin(fr[1:2] * (hdot(h1, w2_ref[...]) + b2_ref[...]))
    decay = jnp.exp(-z[:, 0:1] * dl_ref[...])
    hf = (hdot(h2, w3f_ref[...]) + b3f_ref[...]) * decay
    hb = (hdot(h2, w3b_ref[...]) + b3b_ref[...]) * decay
    inv = 1.0 / (jnp.sum(jnp.abs(hf), axis=0, keepdims=True) + jnp.sum(jnp.abs(hb), axis=0, keepdims=True))
    hf_ref[0] = hf * inv
    hb_ref[0] = hb * inv


def _hyena_filters(seq, p, tn=256):
    d = D_MODEL
    width = p['w2'].shape[0]
    t = np.linspace(0.0, 1.0, seq)[:, None]
    bands = (HY_EMB_DIM - 1) // 2
    ang = (2.0 * math.pi * np.arange(seq) / seq)[:, None] * np.linspace(1e-4, bands - 1, bands)[None, :]
    feats = np.zeros((seq, LANES), np.float32)
    feats[:, :HY_EMB_DIM] = np.concatenate([t, np.cos(ang), -np.sin(ang)], axis=-1)
    w1 = jnp.zeros((LANES, width), F32).at[:HY_EMB_DIM].set(p['w1'])
    deltas = np.abs(np.linspace(HY_MIN_DECAY, HY_MAX_DECAY, d)).astype(np.float32)[None, :]
    nj = d // tn
    full = lambda shape: pl.BlockSpec(shape, lambda n, j: (0,) * len(shape))
    w3 = lambda dd: pl.BlockSpec((width, tn), lambda n, j: (0, (2 * n + dd) * nj + j))
    b3 = lambda dd: pl.BlockSpec((1, tn), lambda n, j: (0, (2 * n + dd) * nj + j))
    out = jax.ShapeDtypeStruct((HY_ORDER, seq, d), F32)
    oblk = pl.BlockSpec((1, seq, tn), lambda n, j: (n, 0, j))
    b3row = p['b3'].reshape(1, -1)
    return pl.pallas_call(
        _hy_filter_kernel, out_shape=(out, out), grid=(HY_ORDER, nj),
        in_specs=[full((seq, LANES)), full((LANES, width)), full((1, width)), full((2, width)),
                  full((width, width)), full((1, width)), w3(0), b3(0), w3(1), b3(1),
                  pl.BlockSpec((1, tn), lambda n, j: (0, j))],
        out_specs=(oblk, oblk), compiler_params=_params("arbitrary", "arbitrary"), name="hyena_filters",
    )(jnp.asarray(feats), w1, p['b1'].reshape(1, width), p['freq'], p['w2'], p['b2'].reshape(1, width),
      p['w3'], b3row, p['w3'], b3row, jnp.asarray(deltas))


def _dft_matrices(seq):
    f = lax.broadcasted_iota(jnp.int32, (seq, seq), 0)
    t = lax.broadcasted_iota(jnp.int32, (seq, seq), 1)
    ang = ((f * t) % (2 * seq)).astype(F32) * (math.pi / seq)
    nyq = (1 - 2 * (t % 2)).astype(F32)
    fwd = jnp.stack([jnp.cos(ang), jnp.where(f == 0, nyq, -jnp.sin(ang))])
    wgt = jnp.where(lax.broadcasted_iota(jnp.int32, (1, 1, seq), 2) == 0, 0.5 / seq, 1.0 / seq)
    inv = jnp.transpose(fwd, (0, 2, 1)) * wgt
    return fwd.astype(BF16), inv.astype(BF16)


def _hy_spectrum_kernel(f_ref, hf_ref, hb_ref, hr_ref, hi_ref, t_ref):
    hf = hf_ref[0]
    row = lax.broadcasted_iota(jnp.int32, hf.shape, 0)
    hb = jnp.where(row == 0, 0.0, hb_ref[0])
    hs = (hf + hb).astype(BF16)
    hd = (hf - hb).astype(BF16)
    hr = _dot(f_ref[0], hs)
    hi = _dot(f_ref[1], hd)
    nyq = _dot(f_ref[1, 0:8, :], hs)[0:1]
    orow = lax.broadcasted_iota(jnp.int32, hr.shape, 0)
    first = jnp.logical_and(pl.program_id(0) == 0, orow == 0)
    hr_ref[0] = hr
    hi_ref[0] = jnp.where(first, 0.0, hi)
    t_ref[0] = jnp.where(first, nyq, hr)


def _hyena_spectrum(hf, hb, fwd, fm=1024, tn=256):
    n_ord, seq, d = hf.shape
    fm = min(fm, seq)
    out = jax.ShapeDtypeStruct((n_ord, seq, d), F32)
    hblk = pl.BlockSpec((1, seq, tn), lambda c, n, j: (n, 0, j))
    oblk = pl.BlockSpec((1, fm, tn), lambda c, n, j: (n, c, j))
    return pl.pallas_call(
        _hy_spectrum_kernel, out_shape=(out, out, out), grid=(seq // fm, n_ord, d // tn),
        in_specs=[pl.BlockSpec((2, fm, seq), lambda c, n, j: (0, c, 0)), hblk, hblk],
        out_specs=(oblk, oblk, oblk),
        compiler_params=_params("arbitrary", "arbitrary", "arbitrary"), name="hyena_spectrum",
    )(fwd, hf, hb)


def _hy_fwd_kernel(f_ref, z_ref, hr_ref, hi_ref, t_ref, y_ref):
    u = z_ref[0]
    xr = _dot(f_ref[0], u)
    xi = _dot(f_ref[1], u)
    hr, hi, tt = hr_ref[0], hi_ref[0], t_ref[0]
    y_ref[0, 0] = (xr * hr - xi * hi).astype(y_ref.dtype)
    y_ref[0, 1] = (xr * hi + xi * tt).astype(y_ref.dtype)


def _hyena_fwd(z3, z_col0, fwd, hr, hi, tt, order, fm=1024, tn=256):
    batch, seq, _ = z3.shape
    d = D_MODEL
    fm = min(fm, seq)
    zoff = z_col0 // tn
    hblk = pl.BlockSpec((1, fm, tn), lambda c, b, j: (order, c, j))
    return pl.pallas_call(
        _hy_fwd_kernel, out_shape=jax.ShapeDtypeStruct((batch, 2, seq, d), BF16),
        grid=(seq // fm, batch, d // tn),
        in_specs=[pl.BlockSpec((2, fm, seq), lambda c, b, j: (0, c, 0)),
                  pl.BlockSpec((1, seq, tn), lambda c, b, j: (b, 0, zoff + j)), hblk, hblk, hblk],
        out_specs=pl.BlockSpec((1, 2, fm, tn), lambda c, b, j: (b, 0, c, j)),
        compiler_params=_params("arbitrary", "arbitrary", "arbitrary"), name="hyena_fwd_dft",
    )(fwd, z3, hr, hi, tt)


def _hy_inv_kernel(g_ref, y_ref, z_ref, gate_ref, skip_ref, o_ref):
    y = _dot(g_ref[0], y_ref[0, 0]) + _dot(g_ref[1], y_ref[0, 1])
    conv = y + z_ref[0].astype(F32) * skip_ref[...]
    o_ref[0] = (gate_ref[0].astype(F32) * conv).astype(o_ref.dtype)


def _hyena_inv(y, inv, z3, z_col0, gate3, gate_col0, skip, tmc=1024, tn=256):
    batch, _, seq, d = y.shape
    tmc = min(tmc, seq)
    zoff, goff = z_col0 // tn, gate_col0 // tn
    return pl.pallas_call(
        _hy_inv_kernel, out_shape=jax.ShapeDtypeStruct((batch, seq, d), BF16),
        grid=(seq // tmc, batch, d // tn),
        in_specs=[pl.BlockSpec((2, tmc, seq), lambda c, b, j: (0, c, 0)),
                  pl.BlockSpec((1, 2, seq, tn), lambda c, b, j: (b, 0, 0, j)),
                  pl.BlockSpec((1, tmc, tn), lambda c, b, j: (b, c, zoff + j)),
                  pl.BlockSpec((1, tmc, tn), lambda c, b, j: (b, c, goff + j)),
                  pl.BlockSpec((1, tn), lambda c, b, j: (0, j))],
        out_specs=pl.BlockSpec((1, tmc, tn), lambda c, b, j: (b, c, j)),
        compiler_params=_params("arbitrary", "arbitrary", "arbitrary"), name="hyena_inv_dft",
    )(inv, y, z3, gate3, skip.reshape(1, d))


def _hyena_branch(h, batch, p):
    d = D_MODEL
    m = h.shape[0]
    seq = m // batch
    xs = _dwconv3(_matmul(h, p['w_in']), p['conv'], batch).reshape(batch, seq, 3 * d)
    hf, hb = _hyena_filters(seq, p)
    fwd, inv = _dft_matrices(seq)
    hr, hi, tt = _hyena_spectrum(hf, hb, fwd)
    z, z_col0 = xs, 2 * d
    for n in range(HY_ORDER):
        y = _hyena_fwd(z, z_col0, fwd, hr, hi, tt, n)
        z, z_col0 = _hyena_inv(y, inv, z, z_col0, xs, n * d, p['skip'][n]), 0
    return z.reshape(m, d)


DENSE_FFN_PARTS = 2


def kernel(x, c, ctx, c_ctx, ada_w, ada_b, norm_g, na_w_qkv, na_q_norm, na_k_norm, na_rpb, na_w_o, gdn_w_qkv, gdn_conv, gdn_w_gate, gdn_w_beta, gdn_w_decay, gdn_a_log, gdn_dt_bias, gdn_out_norm, gdn_w_o, diff_w_qkv, diff_q_norm, diff_k_norm, diff_lambda, diff_out_norm, diff_w_o, hy_w_in, hy_conv, hy_filt_w1, hy_filt_b1, hy_filt_freq, hy_filt_w2, hy_filt_b2, hy_filt_w3, hy_filt_b3, hy_skip, hy_w_o, ffn_w_in, ffn_w_out, moe_router, moe_w_in, moe_w_out):
    batch, seq, d = x.shape
    n_ctx = ctx.shape[1]
    depth = ada_w.shape[0]
    bf = lambda w: w.astype(BF16)

    n_rows = -(-(batch + 1) // 8) * 8
    cc = jnp.zeros((n_rows, d), F32).at[:batch].set(c).at[batch].set(c_ctx)
    mods = _ada_mods(cc, ada_w, ada_b)

    xl = x.reshape(batch * seq, d)
    xc = ctx.reshape(batch * n_ctx, d)
    mods_l = [[mods[i, :batch, k * d:(k + 1) * d].reshape(batch, 1, d) for k in range(6)] for i in range(depth)]
    mods_c = [[mods[i, batch:batch + 1, k * d:(k + 1) * d].reshape(1, 1, d) for k in range(6)]
              for i in range(depth)]
    ctx_used = [i != depth - 1 or i % 4 != 3 for i in range(depth)]
    hl = _norm_modulate(xl, norm_g[0, 0], mods_l[0][0], mods_l[0][1])
    hc = _norm_modulate(xc, norm_g[0, 0], mods_c[0][0], mods_c[0][1]) if ctx_used[0] else None
    for i in range(depth):
        last = i == depth - 1
        kind = i % 4
        ml, mc = mods_l[i], mods_c[i]
        ctx_needed = ctx_used[i]
        oc = None
        if kind == 0:
            w = bf(na_w_qkv)
            qkv, qkv_c = _matmul(hl, w), _matmul(hc, w)
            gq = jnp.tile(na_q_norm, NA_HEADS) * NA_HEAD_DIM ** -0.5
            gk = jnp.tile(na_k_norm, NA_HEADS)
            ol, oc = _neighbourhood_attention(
                _head_norm(qkv, 0, gq, NA_HEAD_DIM), _head_norm(qkv, 1, gk, NA_HEAD_DIM), qkv,
                _head_norm(qkv_c, 0, gq, NA_HEAD_DIM), _head_norm(qkv_c, 1, gk, NA_HEAD_DIM), qkv_c,
                na_rpb, batch)
            w_o = bf(na_w_o)
        elif kind == 1:
            p = dict(w_qkv=bf(gdn_w_qkv), conv=gdn_conv, w_gate=bf(gdn_w_gate), w_beta=gdn_w_beta,
                     w_decay=gdn_w_decay, a_log=gdn_a_log, dt_bias=gdn_dt_bias, out_norm=gdn_out_norm)
            zeros = jnp.zeros((batch, 2, GDN_HEADS, GDN_HEAD_DIM, GDN_HEAD_DIM), F32)
            oc, s_ctx = _gdn_branch(hc, batch, zeros, p)
            ol, _ = _gdn_branch(hl, batch, s_ctx, p)
            w_o = bf(gdn_w_o)
        elif kind == 2:
            lambda_init = 0.8 - 0.6 * math.exp(-0.3 * i)
            w = bf(diff_w_qkv)
            qkv, qkv_c = _matmul(hl, w), _matmul(hc, w)
            reps = d // DIFF_HEAD_DIM
            gq = jnp.tile(diff_q_norm, reps) * DIFF_HEAD_DIM ** -0.5
            gk = jnp.tile(diff_k_norm, reps)
            ol, oc = _diff_attention(
                _head_norm(qkv, 0, gq, DIFF_HEAD_DIM, seq=seq, rope=True),
                _head_norm(qkv, 1, gk, DIFF_HEAD_DIM, seq=seq, rope=True), qkv,
                _head_norm(qkv_c, 0, gq, DIFF_HEAD_DIM), _head_norm(qkv_c, 1, gk, DIFF_HEAD_DIM), qkv_c,
                diff_lambda, diff_out_norm, lambda_init, batch)
            w_o = bf(diff_w_o)
        else:
            p = dict(w_in=bf(hy_w_in), conv=hy_conv, w1=hy_filt_w1, b1=hy_filt_b1, freq=hy_filt_freq,
                     w2=hy_filt_w2, b2=hy_filt_b2, w3=hy_filt_w3, b3=hy_filt_b3, skip=hy_skip)
            ol = _hyena_branch(hl, batch, p)
            oc = _hyena_branch(hc, batch, p) if ctx_needed else None
            w_o = bf(hy_w_o)
        j = i // 2
        dense = i % 2 == 0
        streams = [(xl, ol, ml, None if last else mods_l[i + 1])]
        if not last:
            streams.append((xc, oc, mc, mods_c[i + 1] if ctx_used[i + 1] else None))
        outs = []
        for xs, os_, ms, ms_next in streams:
            nxt = None if ms_next is None else (norm_g[i + 1, 0], ms_next[0], ms_next[1])
            if dense:
                xs, hs = _matmul_residual(os_, w_o, xs, ms[2], norm=(norm_g[i, 1], ms[3], ms[4]))
                w_in = bf(ffn_w_in[j]).reshape(1, d, -1)
                w_out = bf(ffn_w_out[j]).reshape(DENSE_FFN_PARTS, -1, d)
                res = _dense_ffn(hs, w_in, w_out, xs, ms[5], norm=nxt)
            else:
                xs = _matmul_residual(os_, w_o, xs, ms[2])
                res = _moe(xs, norm_g[i, 1], ms[3], ms[4], ms[5], moe_router[j],
                           bf(moe_w_in[j]), bf(moe_w_out[j]), norm=nxt)
            outs.append(res if nxt is not None else (res, None))
        xl, hl = outs[0]
        if not last:
            xc, hc = outs[1]
    return xl.reshape(batch, seq, d)
```
